```python
import jax, jax.numpy as jnp
from jax import lax
import numpy as np

D_MODEL = 1024
BATCH = 8
SEQ = 4096
DEPTH = 2
DEC_BATCH = 32
DEC_SEQ = 16
PAST_LEN = 2048

CHUNK = 64
N_EVEN = (DEPTH + 1) // 2
N_ODD = DEPTH // 2
RMS_EPS = 1e-6
LN_EPS = 1e-5
CONF_DIM = 512
CONF_WIDTH = 31
SSD_HEADS = 8
SSD_HEAD_DIM = 64
SSD_DIM = SSD_HEADS * SSD_HEAD_DIM
SSD_STATE = 128
SSD_GROUPS = 2
SSD_CONV = 4
SSD_XBC = SSD_DIM + 2 * SSD_GROUPS * SSD_STATE
SSD_CHUNK = CHUNK
EV_PROJ = 2 * CONF_DIM + SSD_DIM + SSD_XBC + SSD_HEADS
EV_MIX = CONF_DIM + SSD_DIM
SC_DIM = 512
SC_WIDTH = 3
FOX_HEADS = 8
FOX_HEAD_DIM = 64
FOX_DIM = FOX_HEADS * FOX_HEAD_DIM
Q_BLOCK = 128
OD_PROJ = 3 * SC_DIM + 3 * FOX_DIM + FOX_HEADS
OD_MIX = SC_DIM + FOX_DIM
MEM_TOKENS = 256
MEM_HEADS = 4
MEM_HEAD_DIM = 128
MEM_DIM = MEM_HEADS * MEM_HEAD_DIM
FFN_DIM = -(-8 * D_MODEL // (3 * 256)) * 256

kernel_name = 'hybrid_stream_encoder_step'


def rmsnorm(x, g, eps=RMS_EPS):
    xf = x.astype(jnp.float32)
    y = xf * lax.rsqrt(jnp.mean(xf * xf, axis=-1, keepdims=True) + eps)
    return (y * g.astype(jnp.float32)).astype(x.dtype)


def layernorm(x, g, b, eps=LN_EPS):
    xf = x.astype(jnp.float32)
    mu = jnp.mean(xf, axis=-1, keepdims=True)
    xc = xf - mu
    var = jnp.mean(xc * xc, axis=-1, keepdims=True)
    return (xc * lax.rsqrt(var + eps) * g.astype(jnp.float32) + b.astype(jnp.float32)).astype(x.dtype)


def causal_dwconv(x, prev, w, b=None):
    width, ch = w.shape
    xp = jnp.concatenate([prev.astype(x.dtype), x], axis=1)
    y = lax.conv_general_dilated(xp, w.astype(x.dtype)[:, None, :], (1,), 'VALID',
                                 dimension_numbers=('NWC', 'WIO', 'NWC'), feature_group_count=ch)
    if b is not None:
        y = y + b.astype(y.dtype)
    return y, xp[:, xp.shape[1] - (width - 1):]


def ssd_scan(x, dt, a, b_in, c_out, h0):
    bsz, seqlen, nh, hd = x.shape
    q = min(SSD_CHUNK, seqlen)
    nc = seqlen // q
    f32 = jnp.float32
    xdt = (x.astype(f32) * dt[..., None]).reshape(bsz, nc, q, nh, hd)
    da = (dt * a).reshape(bsz, nc, q, nh)
    bh = jnp.repeat(b_in.astype(f32), nh // SSD_GROUPS, axis=2).reshape(bsz, nc, q, nh, SSD_STATE)
    chh = jnp.repeat(c_out.astype(f32), nh // SSD_GROUPS, axis=2).reshape(bsz, nc, q, nh, SSD_STATE)
    cs = jnp.cumsum(da, axis=2)
    causal = jnp.tril(jnp.ones((q, q), dtype=bool))
    seg = cs[:, :, :, None, :] - cs[:, :, None, :, :]
    lmat = jnp.exp(jnp.where(causal[None, None, :, :, None], seg, -jnp.inf))
    scores = jnp.einsum('bclhn,bcshn->bclsh', chh, bh) * lmat
    y_diag = jnp.einsum('bclsh,bcshp->bclhp', scores, xdt)
    decay_to_end = jnp.exp(cs[:, :, -1:, :] - cs)
    chunk_states = jnp.einsum('bcshn,bcsh,bcshp->bchpn', bh, decay_to_end, xdt)
    chunk_decay = jnp.exp(cs[:, :, -1, :])

    def step(state, inp):
        st, dec = inp
        return state * dec[:, :, None, None] + st, state

    h_last, h_in = lax.scan(step, h0.astype(f32),
                            (jnp.moveaxis(chunk_states, 1, 0), jnp.moveaxis(chunk_decay, 1, 0)))
    h_in = jnp.moveaxis(h_in, 0, 1)
    y_off = jnp.einsum('bclhn,bchpn,bclh->bclhp', chh, h_in, jnp.exp(cs))
    y = (y_diag + y_off).reshape(bsz, seqlen, nh, hd)
    return y.astype(x.dtype), h_last.astype(h0.dtype)


def mixer_even(h, conf_state, ssd_conv_state, ssd_state, w_in, conf_w, conf_b, ln_g, ln_b,
               xbc_w, xbc_b, dt_bias, a_log, d_skip, gnorm_g, w_out):
    bsz, seqlen, _ = h.shape
    proj = h @ w_in
    a_val, a_gate, z, xbc, dt_raw = jnp.split(
        proj, [CONF_DIM, 2 * CONF_DIM, 2 * CONF_DIM + SSD_DIM, 2 * CONF_DIM + SSD_DIM + SSD_XBC], axis=-1)
    a_glu = a_val * jax.nn.sigmoid(a_gate)
    a_conv, new_conf_state = causal_dwconv(a_glu, conf_state, conf_w, conf_b)
    a_out = jax.nn.silu(layernorm(a_conv, ln_g, ln_b))
    xbc_c, new_xbc_state = causal_dwconv(xbc, ssd_conv_state, xbc_w, xbc_b)
    xbc_c = jax.nn.silu(xbc_c)
    xs, b_in, c_out = jnp.split(xbc_c, [SSD_DIM, SSD_DIM + SSD_GROUPS * SSD_STATE], axis=-1)
    xs = xs.reshape(bsz, seqlen, SSD_HEADS, SSD_HEAD_DIM)
    b_in = b_in.reshape(bsz, seqlen, SSD_GROUPS, SSD_STATE)
    c_out = c_out.reshape(bsz, seqlen, SSD_GROUPS, SSD_STATE)
    dt = jax.nn.softplus(dt_raw.astype(jnp.float32) + dt_bias.astype(jnp.float32))
    a = -jnp.exp(a_log.astype(jnp.float32))
    y, new_ssd_state = ssd_scan(xs, dt, a, b_in, c_out, ssd_state)
    y = y + d_skip.astype(y.dtype)[:, None] * xs
    yg = (y.reshape(bsz, seqlen, SSD_DIM) * jax.nn.silu(z)).reshape(bsz, seqlen, SSD_GROUPS, SSD_DIM // SSD_GROUPS)
    b_out = rmsnorm(yg, gnorm_g.reshape(SSD_GROUPS, SSD_DIM // SSD_GROUPS)).reshape(bsz, seqlen, SSD_DIM)
    out = jnp.concatenate([a_out, b_out], axis=-1) @ w_out
    return out, new_conf_state, new_xbc_state, new_ssd_state


def fox_attention(q, k, v, f_q, f_k, q_start):
    bsz, lq, nh, hd = q.shape
    lk = k.shape[1]
    qb = min(Q_BLOCK, lq)
    nb = lq // qb
    scale = hd ** -0.5
    k_pos = jnp.arange(lk)
    f_kt = jnp.swapaxes(f_k, 1, 2)

    def block(args):
        q_blk, fq_blk, pos_blk = args
        s = jnp.einsum('bqhd,bkhd->bhqk', q_blk, k).astype(jnp.float32) * scale
        s = s + jnp.swapaxes(fq_blk, 1, 2)[..., None] - f_kt[:, :, None, :]
        s = jnp.where(k_pos[None, None, None, :] <= pos_blk[None, None, :, None], s, -jnp.inf)
        p = jax.nn.softmax(s, axis=-1)
        return jnp.einsum('bhqk,bkhd->bqhd', p.astype(v.dtype), v)

    q_blocks = jnp.moveaxis(q.reshape(bsz, nb, qb, nh, hd), 1, 0)
    fq_blocks = jnp.moveaxis(f_q.reshape(bsz, nb, qb, nh), 1, 0)
    pos_blocks = (q_start + jnp.arange(lq)).reshape(nb, qb)
    out = lax.map(block, (q_blocks, fq_blocks, pos_blocks))
    return jnp.moveaxis(out, 0, 1).reshape(bsz, lq, nh, hd)


def mixer_odd(h, sconv_state, past_k, past_v, past_logf, w_in, sconv_w, q_norm_g, k_norm_g, f_bias, w_out):
    bsz, seqlen, _ = h.shape
    proj = h @ w_in
    u, gate_b, gate_c, q, k, v, f_raw = jnp.split(
        proj, [SC_DIM, 2 * SC_DIM, 3 * SC_DIM, 3 * SC_DIM + FOX_DIM, 3 * SC_DIM + 2 * FOX_DIM,
               3 * SC_DIM + 3 * FOX_DIM], axis=-1)
    conv, new_sconv_state = causal_dwconv(gate_c * u, sconv_state, sconv_w)
    c_out = gate_b * conv
    q = rmsnorm(q.reshape(bsz, seqlen, FOX_HEADS, FOX_HEAD_DIM), q_norm_g)
    k = rmsnorm(k.reshape(bsz, seqlen, FOX_HEADS, FOX_HEAD_DIM), k_norm_g)
    v = v.reshape(bsz, seqlen, FOX_HEADS, FOX_HEAD_DIM)
    logf = jax.nn.log_sigmoid(f_raw.astype(jnp.float32) + f_bias.astype(jnp.float32))
    past_len = past_k.shape[1]
    k_all = jnp.concatenate([past_k.astype(k.dtype), k], axis=1)
    v_all = jnp.concatenate([past_v.astype(v.dtype), v], axis=1)
    f_cum = jnp.cumsum(jnp.concatenate([past_logf.astype(jnp.float32), logf], axis=1), axis=1)
    d_out = fox_attention(q, k_all, v_all, f_cum[:, past_len:], f_cum, past_len)
    out = jnp.concatenate([c_out, d_out.reshape(bsz, seqlen, FOX_DIM)], axis=-1) @ w_out
    return out, new_sconv_state, k, v, logf.astype(h.dtype)


def mem_kv(mem, norm_g, wk, wv, k_norm_g):
    bsz, m, _ = mem.shape
    hm = rmsnorm(mem, norm_g)
    mk = rmsnorm((hm @ wk).reshape(bsz, m, MEM_HEADS, MEM_HEAD_DIM), k_norm_g)
    mv = (hm @ wv).reshape(bsz, m, MEM_HEADS, MEM_HEAD_DIM)
    return mk, mv


def cross_attn(h, wq, q_norm_g, mk, mv, wo):
    bsz, seqlen, _ = h.shape
    q = rmsnorm((h @ wq).reshape(bsz, seqlen, MEM_HEADS, MEM_HEAD_DIM), q_norm_g)
    s = jnp.einsum('blhd,bmhd->bhlm', q, mk.astype(q.dtype)).astype(jnp.float32) * (MEM_HEAD_DIM ** -0.5)
    p = jax.nn.softmax(s, axis=-1)
    o = jnp.einsum('bhlm,bmhd->blhd', p.astype(h.dtype), mv.astype(h.dtype))
    return o.reshape(bsz, seqlen, MEM_DIM) @ wo


def swiglu(h, w1, w3, w2):
    return (jax.nn.silu(h @ w1) * (h @ w3)) @ w2


def setup_inputs(seed: int = 0) -> dict:
    key = jax.random.key(seed)
    ks = iter(jax.random.split(key, 64))

    def nrm(shape, scale):
        return jax.random.normal(next(ks), shape, jnp.float32) * scale

    def gain(shape):
        return 1.0 + nrm(shape, 0.05)

    dt0 = jnp.exp(jax.random.uniform(next(ks), (N_EVEN, SSD_HEADS), jnp.float32,
                                     minval=float(np.log(1e-3)), maxval=float(np.log(1e-1))))
    return {
        'x_prompt': nrm((BATCH, SEQ, D_MODEL), 1.0),
        'x_sample': nrm((DEC_BATCH, DEC_SEQ, D_MODEL), 1.0),
        'state_conf_conv': nrm((N_EVEN, DEC_BATCH, CONF_WIDTH - 1, CONF_DIM), 0.5),
        'state_ssd_conv': nrm((N_EVEN, DEC_BATCH, SSD_CONV - 1, SSD_XBC), 0.5),
        'state_ssd': nrm((N_EVEN, DEC_BATCH, SSD_HEADS, SSD_HEAD_DIM, SSD_STATE), 0.1),
        'state_sconv': nrm((N_ODD, DEC_BATCH, SC_WIDTH - 1, SC_DIM), 0.5),
        'cache_fox_k': nrm((N_ODD, DEC_BATCH, PAST_LEN, FOX_HEADS, FOX_HEAD_DIM), 1.0),
        'cache_fox_v': nrm((N_ODD, DEC_BATCH, PAST_LEN, FOX_HEADS, FOX_HEAD_DIM), 1.0),
        'cache_fox_logf': jax.nn.log_sigmoid(2.5 + nrm((N_ODD, DEC_BATCH, PAST_LEN, FOX_HEADS), 1.0)),
        'cache_mem_k': nrm((DEPTH, DEC_BATCH, MEM_TOKENS, MEM_HEADS, MEM_HEAD_DIM), 1.0),
        'cache_mem_v': nrm((DEPTH, DEC_BATCH, MEM_TOKENS, MEM_HEADS, MEM_HEAD_DIM), 1.0),
        'mem_prompt': nrm((BATCH, MEM_TOKENS, D_MODEL), 1.0),
        'norm_mix_g': gain((DEPTH, D_MODEL)),
        'norm_cross_g': gain((DEPTH, D_MODEL)),
        'norm_mem_g': gain((DEPTH, D_MODEL)),
        'norm_ffn_g': gain((DEPTH, D_MODEL)),
        'ev_w_in': nrm((N_EVEN, D_MODEL, EV_PROJ), D_MODEL ** -0.5),
        'ev_w_out': nrm((N_EVEN, EV_MIX, D_MODEL), EV_MIX ** -0.5),
        'conf_dw_w': nrm((N_EVEN, CONF_WIDTH, CONF_DIM), CONF_WIDTH ** -0.5),
        'conf_dw_b': nrm((N_EVEN, CONF_DIM), 0.02),
        'conf_ln_g': gain((N_EVEN, CONF_DIM)),
        'conf_ln_b': nrm((N_EVEN, CONF_DIM), 0.02),
        'ssd_conv_w': nrm((N_EVEN, SSD_CONV, SSD_XBC), SSD_CONV ** -0.5),
        'ssd_conv_b': nrm((N_EVEN, SSD_XBC), 0.02),
        'ssd_dt_bias': dt0 + jnp.log(-jnp.expm1(-dt0)),
        'ssd_a_log': jnp.log(jax.random.uniform(next(ks), (N_EVEN, SSD_HEADS), jnp.float32, minval=1.0, maxval=16.0)),
        'ssd_d': gain((N_EVEN, SSD_HEADS)),
        'ssd_norm_g': gain((N_EVEN, SSD_DIM)),
        'od_w_in': nrm((N_ODD, D_MODEL, OD_PROJ), D_MODEL ** -0.5),
        'od_w_out': nrm((N_ODD, OD_MIX, D_MODEL), OD_MIX ** -0.5),
        'sconv_w': nrm((N_ODD, SC_WIDTH, SC_DIM), SC_WIDTH ** -0.5),
        'fox_q_norm_g': gain((N_ODD, FOX_HEAD_DIM)),
        'fox_k_norm_g': gain((N_ODD, FOX_HEAD_DIM)),
        'fox_f_bias': jax.random.uniform(next(ks), (N_ODD, FOX_HEADS), jnp.float32, minval=1.0, maxval=4.0),
        'xa_wq': nrm((DEPTH, D_MODEL, MEM_DIM), D_MODEL ** -0.5),
        'xa_wk': nrm((DEPTH, D_MODEL, MEM_DIM), D_MODEL ** -0.5),
        'xa_wv': nrm((DEPTH, D_MODEL, MEM_DIM), D_MODEL ** -0.5),
        'xa_wo': nrm((DEPTH, MEM_DIM, D_MODEL), MEM_DIM ** -0.5),
        'xa_q_norm_g': gain((DEPTH, MEM_HEAD_DIM)),
        'xa_k_norm_g': gain((DEPTH, MEM_HEAD_DIM)),
        'ffn_w1': nrm((DEPTH, D_MODEL, FFN_DIM), D_MODEL ** -0.5),
        'ffn_w3': nrm((DEPTH, D_MODEL, FFN_DIM), D_MODEL ** -0.5),
        'ffn_w2': nrm((DEPTH, FFN_DIM, D_MODEL), FFN_DIM ** -0.5),
    }


def reference(x_prompt, x_sample, state_conf_conv, state_ssd_conv, state_ssd, state_sconv,
              cache_fox_k, cache_fox_v, cache_fox_logf, cache_mem_k, cache_mem_v, mem_prompt,
              norm_mix_g, norm_cross_g, norm_mem_g, norm_ffn_g,
              ev_w_in, ev_w_out, conf_dw_w, conf_dw_b, conf_ln_g, conf_ln_b,
              ssd_conv_w, ssd_conv_b, ssd_dt_bias, ssd_a_log, ssd_d, ssd_norm_g,
              od_w_in, od_w_out, sconv_w, fox_q_norm_g, fox_k_norm_g, fox_f_bias,
              xa_wq, xa_wk, xa_wv, xa_wo, xa_q_norm_g, xa_k_norm_g,
              ffn_w1, ffn_w3, ffn_w2):
    yp, ys = x_prompt, x_sample
    bp, dt_ = x_prompt.shape[0], x_prompt.dtype
    conf_p, conf_s, xbc_p, xbc_s, ssd_p, ssd_s = [], [], [], [], [], []
    sc_p, sc_s, fk_p, fk_s, fv_p, fv_s, fl_p, fl_s = [], [], [], [], [], [], [], []
    mk_out, mv_out = [], []
    for l in range(DEPTH):
        g = l // 2
        hp = rmsnorm(yp, norm_mix_g[l])
        hs = rmsnorm(ys, norm_mix_g[l])
        if l % 2 == 0:
            w_ev = (ev_w_in[g], conf_dw_w[g], conf_dw_b[g], conf_ln_g[g], conf_ln_b[g], ssd_conv_w[g],
                    ssd_conv_b[g], ssd_dt_bias[g], ssd_a_log[g], ssd_d[g], ssd_norm_g[g], ev_w_out[g])
            mp, c1, c2, c3 = mixer_even(hp, jnp.zeros((bp, CONF_WIDTH - 1, CONF_DIM), dt_),
                                        jnp.zeros((bp, SSD_CONV - 1, SSD_XBC), dt_),
                                        jnp.zeros((bp, SSD_HEADS, SSD_HEAD_DIM, SSD_STATE), dt_), *w_ev)
            ms, s1, s2, s3 = mixer_even(hs, state_conf_conv[g], state_ssd_conv[g], state_ssd[g], *w_ev)
            conf_p.append(c1); xbc_p.append(c2); ssd_p.append(c3)
            conf_s.append(s1); xbc_s.append(s2); ssd_s.append(s3)
        else:
            w_od = (od_w_in[g], sconv_w[g], fox_q_norm_g[g], fox_k_norm_g[g], fox_f_bias[g], od_w_out[g])
            mp, c1, c2, c3, c4 = mixer_odd(hp, jnp.zeros((bp, SC_WIDTH - 1, SC_DIM), dt_),
                                           jnp.zeros((bp, 0, FOX_HEADS, FOX_HEAD_DIM), dt_),
                                           jnp.zeros((bp, 0, FOX_HEADS, FOX_HEAD_DIM), dt_),
                                           jnp.zeros((bp, 0, FOX_HEADS), dt_), *w_od)
            ms, s1, s2, s3, s4 = mixer_odd(hs, state_sconv[g], cache_fox_k[g], cache_fox_v[g],
                                           cache_fox_logf[g], *w_od)
            sc_p.append(c1); fk_p.append(c2); fv_p.append(c3); fl_p.append(c4)
            sc_s.append(s1); fk_s.append(s2); fv_s.append(s3); fl_s.append(s4)
        yp = yp + mp
        ys = ys + ms
        mkp, mvp = mem_kv(mem_prompt, norm_mem_g[l], xa_wk[l], xa_wv[l], xa_k_norm_g[l])
        mk_out.append(mkp); mv_out.append(mvp)
        yp = yp + cross_attn(rmsnorm(yp, norm_cross_g[l]), xa_wq[l], xa_q_norm_g[l], mkp, mvp, xa_wo[l])
        ys = ys + cross_attn(rmsnorm(ys, norm_cross_g[l]), xa_wq[l], xa_q_norm_g[l],
                             cache_mem_k[l], cache_mem_v[l], xa_wo[l])
        yp = yp + swiglu(rmsnorm(yp, norm_ffn_g[l]), ffn_w1[l], ffn_w3[l], ffn_w2[l])
        ys = ys + swiglu(rmsnorm(ys, norm_ffn_g[l]), ffn_w1[l], ffn_w3[l], ffn_w2[l])
    new_conf_conv_p = jnp.stack(conf_p)
    new_conf_conv_s = jnp.stack(conf_s)
    new_ssd_conv_p = jnp.stack(xbc_p)
    new_ssd_conv_s = jnp.stack(xbc_s)
    new_ssd_p = jnp.stack(ssd_p)
    new_ssd_s = jnp.stack(ssd_s)
    new_sconv_p = jnp.stack(sc_p)
    new_sconv_s = jnp.stack(sc_s)
    new_fox_k_p = jnp.stack(fk_p)
    new_fox_k_s = jnp.stack(fk_s)
    new_fox_v_p = jnp.stack(fv_p)
    new_fox_v_s = jnp.stack(fv_s)
    new_fox_logf_p = jnp.stack(fl_p)
    new_fox_logf_s = jnp.stack(fl_s)
    new_mem_k_p = jnp.stack(mk_out)
    new_mem_v_p = jnp.stack(mv_out)
    return (yp, ys, new_conf_conv_p, new_conf_conv_s, new_ssd_conv_p, new_ssd_conv_s, new_ssd_p, new_ssd_s,
            new_sconv_p, new_sconv_s, new_fox_k_p, new_fox_k_s, new_fox_v_p, new_fox_v_s,
            new_fox_logf_p, new_fox_logf_s, new_mem_k_p, new_mem_v_p)
```

```python
import functools

import jax
import jax.numpy as jnp
from jax import lax
from jax.experimental import pallas as pl
from jax.experimental.pallas import tpu as pltpu

F32 = jnp.float32
BF16 = jnp.bfloat16

D_MODEL = 1024
RMS_EPS = 1e-6
LN_EPS = 1e-5
CONF_DIM = 512
CONF_WIDTH = 31
SSD_HEADS = 8
SSD_HEAD_DIM = 64
SSD_DIM = 512
SSD_STATE = 128
SSD_GROUPS = 2
SSD_CONV = 4
SSD_XBC = 1024
SC_DIM = 512
SC_WIDTH = 3
FOX_HEADS = 8
FOX_HEAD_DIM = 64
FOX_DIM = 512
MEM_TOKENS = 256
MEM_HEADS = 4
MEM_HEAD_DIM = 128
MEM_DIM = 512

LANES = 128
SUBLANES = 8
TOKEN_TILE = 512
SSD_Q = 128
VMEM_LIMIT = 56 * 1024 * 1024


def _params(*sem):
    return pltpu.CompilerParams(dimension_semantics=sem, vmem_limit_bytes=VMEM_LIMIT)


def _full(shape):
    nd = len(shape)
    return pl.BlockSpec(shape, lambda *_: (0,) * nd)


def _rms(x, g):
    return x * lax.rsqrt(jnp.mean(x * x, axis=-1, keepdims=True) + RMS_EPS) * g


def _softplus(x):
    return jnp.maximum(x, 0.0) + jnp.log1p(jnp.exp(-jnp.abs(x)))


def _sigmoid(x):
    return 1.0 / (1.0 + jnp.exp(-x))


def _silu(x):
    return x * _sigmoid(x)


def _dot(a, b):
    return jnp.dot(a, b, preferred_element_type=F32)


def _dot_nt(a, b):
    return lax.dot_general(a, b, (((1,), (1,)), ((), ())), preferred_element_type=F32)


def _dot_tn(a, b):
    return lax.dot_general(a, b, (((0,), (0,)), ((), ())), preferred_element_type=F32)


def _split3(x):
    hi = x.astype(BF16)
    r = x - hi.astype(F32)
    mid = r.astype(BF16)
    lo = (r - mid.astype(F32)).astype(BF16)
    return hi, mid, lo


def _cumsum_rows(tril, x):
    hi, mid, lo = _split3(x)
    return _dot(tril, hi) + _dot(tril, mid) + _dot(tril, lo)


def _cumsum_lanes(x, triu):
    hi, mid, lo = _split3(x)
    return _dot(hi, triu) + _dot(mid, triu) + _dot(lo, triu)


def _tri(n, upper=False, block=None):
    r = jnp.arange(n)
    m = (r[:, None] <= r[None, :]) if upper else (r[:, None] >= r[None, :])
    if block is not None:
        m = m & ((r[:, None] // block) == (r[None, :] // block))
    return m.astype(BF16)


def _inproj_even_kernel(x_ref, g_ref, wa_ref, wz_ref, wx_ref, wd_ref, wdt_ref, db_ref, dbc_ref,
                        glu_ref, zs_ref, xbc_ref, dtf_ref, dtr_ref):
    h = _rms(x_ref[...], g_ref[...]).astype(BF16)
    a = _dot(h, wa_ref[...])
    glu_ref[...] = a[:, :CONF_DIM] * _sigmoid(a[:, CONF_DIM:])
    zs_ref[...] = _silu(_dot(h, wz_ref[...])).astype(BF16)
    xbc_ref[...] = _dot(h, wx_ref[...])
    dtf_ref[...] = _softplus(_dot(h, wd_ref[...]) + db_ref[...])
    dtr_ref[...] = _softplus(_dot_nt(wdt_ref[...], h) + dbc_ref[...])


def _inproj_even(x, g, w_in, dt_bias):
    t = x.shape[0]
    tm = min(TOKEN_TILE, t)
    wa = w_in[:, :2 * CONF_DIM].astype(BF16)
    wz = w_in[:, 2 * CONF_DIM:2 * CONF_DIM + SSD_DIM].astype(BF16)
    wx = w_in[:, 2 * CONF_DIM + SSD_DIM:2 * CONF_DIM + SSD_DIM + SSD_XBC].astype(BF16)
    wdt = w_in[:, 2 * CONF_DIM + SSD_DIM + SSD_XBC:]
    wd = jnp.repeat(wdt, SSD_HEAD_DIM, axis=1).astype(BF16)
    db = jnp.repeat(dt_bias.astype(F32), SSD_HEAD_DIM)[None, :]
    row = lambda i: (i, 0)
    return pl.pallas_call(
        _inproj_even_kernel,
        grid=(t // tm,),
        in_specs=[pl.BlockSpec((tm, D_MODEL), row), _full((1, D_MODEL)),
                  _full(wa.shape), _full(wz.shape), _full(wx.shape), _full(wd.shape),
                  _full((SSD_HEADS, D_MODEL)), _full((1, SSD_DIM)), _full((SSD_HEADS, 1))],
        out_specs=[pl.BlockSpec((tm, CONF_DIM), row), pl.BlockSpec((tm, SSD_DIM), row),
                   pl.BlockSpec((tm, SSD_XBC), row), pl.BlockSpec((tm, SSD_DIM), row),
                   pl.BlockSpec((SSD_HEADS, tm), lambda i: (0, i))],
        out_shape=[jax.ShapeDtypeStruct((t, CONF_DIM), F32), jax.ShapeDtypeStruct((t, SSD_DIM), BF16),
                   jax.ShapeDtypeStruct((t, SSD_XBC), F32), jax.ShapeDtypeStruct((t, SSD_DIM), F32),
                   jax.ShapeDtypeStruct((SSD_HEADS, t), F32)],
        compiler_params=_params("parallel"),
        name="inproj_even",
    )(x, g[None, :], wa, wz, wx, wd, wdt.T.astype(BF16), db, dt_bias.astype(F32)[:, None])


CONF_HALO = 32
CONF_ROWS = 32


def _conf_kernel(x_ref, st_ref, w_ref, b_ref, lg_ref, lb_ref, o_ref, buf, *, tl):
    i = pl.program_id(1)
    keep = CONF_WIDTH - 1
    lo = CONF_HALO - keep

    @pl.when(i == 0)
    def _():
        buf[lo:CONF_HALO, :] = st_ref[0]

    @pl.when(i > 0)
    def _():
        buf[lo:CONF_HALO, :] = buf[tl + lo:tl + CONF_HALO, :]

    buf[CONF_HALO:CONF_HALO + tl, :] = x_ref[0]
    rows = min(CONF_ROWS, tl)
    for c in range(tl // rows):
        base = lo + c * rows
        acc = jnp.broadcast_to(b_ref[...], (rows, CONF_DIM))
        for k in range(CONF_WIDTH):
            acc = acc + w_ref[k:k + 1, :] * buf[base + k:base + k + rows, :]
        mu = jnp.mean(acc, axis=-1, keepdims=True)
        xc = acc - mu
        var = jnp.mean(xc * xc, axis=-1, keepdims=True)
        y = xc * lax.rsqrt(var + LN_EPS) * lg_ref[...] + lb_ref[...]
        o_ref[0, c * rows:(c + 1) * rows, :] = _silu(y).astype(BF16)


def _conf_conv(glu, state, w, b, ln_g, ln_b):
    bsz, seqlen, _ = glu.shape
    tl = min(TOKEN_TILE, seqlen)
    return pl.pallas_call(
        functools.partial(_conf_kernel, tl=tl),
        grid=(bsz, seqlen // tl),
        in_specs=[pl.BlockSpec((1, tl, CONF_DIM), lambda bi, i: (bi, i, 0)),
                  pl.BlockSpec((1, CONF_WIDTH - 1, CONF_DIM), lambda bi, i: (bi, 0, 0)),
                  _full((CONF_WIDTH, CONF_DIM)), _full((1, CONF_DIM)), _full((1, CONF_DIM)),
                  _full((1, CONF_DIM))],
        out_specs=pl.BlockSpec((1, tl, CONF_DIM), lambda bi, i: (bi, i, 0)),
        out_shape=jax.ShapeDtypeStruct((bsz, seqlen, CONF_DIM), BF16),
        scratch_shapes=[pltpu.VMEM((CONF_HALO + tl, CONF_DIM), F32)],
        compiler_params=_params("parallel", "arbitrary"),
        name="conf_conv",
    )(glu, state, w, b[None, :], ln_g[None, :], ln_b[None, :])


SSD_HALO = 8


def _ssd_kernel(xbc_ref, dtf_ref, dtr_ref, zs_ref, cst_ref, h0_ref, cw_ref, cb_ref, af_ref, ac_ref,
                df_ref, gn_ref, tril_ref, triu_ref, o_ref, ht_ref, xbuf, state, y_scr, *, q, nq):
    i = pl.program_id(1)
    tl = q * nq
    keep = SSD_CONV - 1
    lo = SSD_HALO - keep
    gw = SSD_DIM // SSD_GROUPS
    hpg = SSD_HEADS // SSD_GROUPS

    @pl.when(i == 0)
    def _():
        xbuf[lo:SSD_HALO, :] = cst_ref[0]
        state[...] = h0_ref[0]

    @pl.when(i > 0)
    def _():
        xbuf[lo:SSD_HALO, :] = xbuf[tl + lo:tl + SSD_HALO, :]

    xbuf[SSD_HALO:SSD_HALO + tl, :] = xbc_ref[0]
    row_id = lax.broadcasted_iota(jnp.int32, (q, q), 0)
    col_id = lax.broadcasted_iota(jnp.int32, (q, q), 1)
    causal = row_id >= col_id

    for c in range(nq):
        r0 = c * q
        conv = jnp.broadcast_to(cb_ref[...], (q, SSD_XBC))
        for k in range(SSD_CONV):
            conv = conv + cw_ref[k:k + 1, :] * xbuf[lo + r0 + k:lo + r0 + k + q, :]
        xc = _silu(conv)
        xs = xc[:, :SSD_DIM]
        b_all = xc[:, SSD_DIM:SSD_DIM + SSD_GROUPS * SSD_STATE].astype(BF16)
        c_all = xc[:, SSD_DIM + SSD_GROUPS * SSD_STATE:].astype(BF16)
        dtf = dtf_ref[0, r0:r0 + q, :]
        cs = _cumsum_rows(tril_ref[...], dtf * af_ref[...])
        cs_row = _cumsum_lanes(dtr_ref[0, :, r0:r0 + q] * ac_ref[...], triu_ref[...])
        cs_last = cs[q - 1:q, :]
        xdt = xs * dtf
        xdt_b = xdt.astype(BF16)
        xdtd = (xdt * jnp.exp(cs_last - cs)).astype(BF16)
        chunk_decay = jnp.exp(cs_last)
        for g in range(SSD_GROUPS):
            bg = b_all[:, g * SSD_STATE:(g + 1) * SSD_STATE]
            cg = c_all[:, g * SSD_STATE:(g + 1) * SSD_STATE]
            cb = _dot_nt(cg, bg)
            st_g = state[:, g * gw:(g + 1) * gw]
            y_off = _dot(cg, st_g.astype(BF16))
            for hh in range(hpg):
                hd = g * hpg + hh
                sl = slice(hd * SSD_HEAD_DIM, (hd + 1) * SSD_HEAD_DIM)
                seg = cs[:, hd * SSD_HEAD_DIM:hd * SSD_HEAD_DIM + 1] - cs_row[hd:hd + 1, :]
                lmat = jnp.exp(jnp.where(causal, seg, -jnp.inf))
                y_scr[:, sl] = _dot((cb * lmat).astype(BF16), xdt_b[:, sl])
            gs = slice(g * gw, (g + 1) * gw)
            y_scr[:, gs] = y_scr[:, gs] + y_off * jnp.exp(cs[:, gs])
            state[:, gs] = st_g * chunk_decay[:, gs] + _dot_tn(bg, xdtd[:, gs])
        y = y_scr[...] + df_ref[...] * xs
        yg = y * zs_ref[0, r0:r0 + q, :].astype(F32)
        for g in range(SSD_GROUPS):
            gs = slice(g * gw, (g + 1) * gw)
            o_ref[0, r0:r0 + q, gs] = _rms(yg[:, gs], gn_ref[:, gs]).astype(BF16)

    @pl.when(i == pl.num_programs(1) - 1)
    def _():
        ht_ref[0] = state[...]


def _ssd(xbc, dtf, dtr, zs, conv_state, h0t, conv_w, conv_b, a_log, d_skip, gnorm_g):
    bsz, seqlen, _ = xbc.shape
    q = min(SSD_Q, seqlen)
    tl = min(TOKEN_TILE, seqlen)
    nq = tl // q
    a = -jnp.exp(a_log.astype(F32))
    af = jnp.repeat(a, SSD_HEAD_DIM)[None, :]
    df = jnp.repeat(d_skip.astype(F32), SSD_HEAD_DIM)[None, :]
    seq = lambda bi, i: (bi, i, 0)
    per_b = lambda bi, i: (bi, 0, 0)
    return pl.pallas_call(
        functools.partial(_ssd_kernel, q=q, nq=nq),
        grid=(bsz, seqlen // tl),
        in_specs=[pl.BlockSpec((1, tl, SSD_XBC), seq), pl.BlockSpec((1, tl, SSD_DIM), seq),
                  pl.BlockSpec((1, SSD_HEADS, tl), lambda bi, i: (bi, 0, i)),
                  pl.BlockSpec((1, tl, SSD_DIM), seq),
                  pl.BlockSpec((1, SSD_CONV - 1, SSD_XBC), per_b),
                  pl.BlockSpec((1, SSD_STATE, SSD_DIM), per_b),
                  _full((SSD_CONV, SSD_XBC)), _full((1, SSD_XBC)), _full((1, SSD_DIM)),
                  _full((SSD_HEADS, 1)), _full((1, SSD_DIM)), _full((1, SSD_DIM)),
                  _full((q, q)), _full((q, q))],
        out_specs=[pl.BlockSpec((1, tl, SSD_DIM), seq), pl.BlockSpec((1, SSD_STATE, SSD_DIM), per_b)],
        out_shape=[jax.ShapeDtypeStruct((bsz, seqlen, SSD_DIM), BF16),
                   jax.ShapeDtypeStruct((bsz, SSD_STATE, SSD_DIM), F32)],
        scratch_shapes=[pltpu.VMEM((SSD_HALO + tl, SSD_XBC), F32),
                        pltpu.VMEM((SSD_STATE, SSD_DIM), F32),
                        pltpu.VMEM((q, SSD_DIM), F32)],
        compiler_params=_params("parallel", "arbitrary"),
        name="ssd",
    )(xbc, dtf, dtr, zs, conv_state, h0t, conv_w, conv_b[None, :], af, a[:, None], df,
      gnorm_g[None, :], _tri(q), _tri(q, upper=True))


def _outproj_kernel(x_ref, a_ref, b_ref, wa_ref, wb_ref, o_ref):
    o_ref[...] = x_ref[...] + _dot(a_ref[...], wa_ref[...]) + _dot(b_ref[...], wb_ref[...])


def _outproj(x, a, b, w_out):
    t = x.shape[0]
    tm = min(TOKEN_TILE, t)
    ka = a.shape[1]
    wa = w_out[:ka].astype(BF16)
    wb = w_out[ka:].astype(BF16)
    row = lambda i: (i, 0)
    return pl.pallas_call(
        _outproj_kernel,
        grid=(t // tm,),
        in_specs=[pl.BlockSpec((tm, D_MODEL), row), pl.BlockSpec((tm, ka), row),
                  pl.BlockSpec((tm, b.shape[1]), row), _full(wa.shape), _full(wb.shape)],
        out_specs=pl.BlockSpec((tm, D_MODEL), row),
        out_shape=jax.ShapeDtypeStruct((t, D_MODEL), F32),
        compiler_params=_params("parallel"),
        name="outproj",
    )(x, a, b, wa, wb)


def _inproj_odd_kernel(x_ref, g_ref, wu_ref, wq_ref, wk_ref, wv_ref, wf_ref, wft_ref, fb_ref, fbc_ref,
                       qg_ref, kg_ref, bd_ref, tril_ref, triu_ref,
                       cu_ref, gb_ref, q_ref, kf_ref, kb_ref, vf_ref, vb_ref, lfr_ref, fc_ref, fr_ref,
                       carry_c, carry_r, *, tiles_per_seq):
    i = pl.program_id(0)

    @pl.when(i % tiles_per_seq == 0)
    def _():
        carry_c[...] = jnp.zeros_like(carry_c)
        carry_r[...] = jnp.zeros_like(carry_r)

    h = _rms(x_ref[...], g_ref[...]).astype(BF16)
    ugg = _dot(h, wu_ref[...])
    cu_ref[...] = ugg[:, 2 * SC_DIM:] * ugg[:, :SC_DIM]
    gb_ref[...] = ugg[:, SC_DIM:2 * SC_DIM].astype(BF16)

    def head_norm(v, gain):
        sq = v * v
        hi = sq.astype(BF16)
        lo = (sq - hi.astype(F32)).astype(BF16)
        ms = (_dot(hi, bd_ref[...]) + _dot(lo, bd_ref[...])) * (1.0 / FOX_HEAD_DIM)
        return v * lax.rsqrt(ms + RMS_EPS) * gain

    qn = head_norm(_dot(h, wq_ref[...]), qg_ref[...])
    q_ref[...] = (qn * (FOX_HEAD_DIM ** -0.5)).astype(BF16)
    kn = head_norm(_dot(h, wk_ref[...]), kg_ref[...])
    kf_ref[...] = kn
    kb_ref[...] = kn.astype(BF16)
    v = _dot(h, wv_ref[...])
    vf_ref[...] = v
    vb_ref[...] = v.astype(BF16)

    lf_col = -_softplus(-(_dot(h, wf_ref[...]) + fb_ref[...]))
    lf_row = -_softplus(-(_dot_nt(wft_ref[...], h) + fbc_ref[...]))
    lfr_ref[...] = lf_row
    tm = lf_col.shape[0]
    f_col = _cumsum_rows(tril_ref[...], lf_col) + carry_c[...]
    f_row = _cumsum_lanes(lf_row, triu_ref[...]) + carry_r[:, 0:1]
    fc_ref[...] = f_col
    fr_ref[...] = f_row
    carry_c[...] = f_col[tm - 1:tm, :]
    carry_r[...] = jnp.broadcast_to(f_row[:, tm - 1:tm], carry_r.shape)


def _inproj_odd(x, g, w_in, q_norm_g, k_norm_g, f_bias, seqlen):
    t = x.shape[0]
    tm = min(TOKEN_TILE, t)
    tiles_per_seq = max(seqlen // tm, 1)
    block = seqlen if seqlen < tm else None
    wu = w_in[:, :3 * SC_DIM].astype(BF16)
    wq = w_in[:, 3 * SC_DIM:3 * SC_DIM + FOX_DIM].astype(BF16)
    wk = w_in[:, 3 * SC_DIM + FOX_DIM:3 * SC_DIM + 2 * FOX_DIM].astype(BF16)
    wv = w_in[:, 3 * SC_DIM + 2 * FOX_DIM:3 * SC_DIM + 3 * FOX_DIM].astype(BF16)
    wfr = w_in[:, 3 * SC_DIM + 3 * FOX_DIM:]
    wf = jnp.pad(wfr, ((0, 0), (0, LANES - FOX_HEADS))).astype(BF16)
    fb = jnp.pad(f_bias.astype(F32), (0, LANES - FOX_HEADS))[None, :]
    hid = jnp.arange(FOX_DIM) // FOX_HEAD_DIM
    bd = (hid[:, None] == hid[None, :]).astype(BF16)
    row = lambda i: (i, 0)
    col = lambda i: (0, i)
    tok = lambda n, dt: jax.ShapeDtypeStruct((t, n), dt)
    return pl.pallas_call(
        functools.partial(_inproj_odd_kernel, tiles_per_seq=tiles_per_seq),
        grid=(t // tm,),
        in_specs=[pl.BlockSpec((tm, D_MODEL), row), _full((1, D_MODEL)),
                  _full(wu.shape), _full(wq.shape), _full(wk.shape), _full(wv.shape), _full(wf.shape),
                  _full((FOX_HEADS, D_MODEL)), _full((1, LANES)), _full((FOX_HEADS, 1)),
                  _full((1, FOX_DIM)), _full((1, FOX_DIM)), _full((FOX_DIM, FOX_DIM)),
                  _full((tm, tm)), _full((tm, tm))],
        out_specs=[pl.BlockSpec((tm, SC_DIM), row), pl.BlockSpec((tm, SC_DIM), row),
                   pl.BlockSpec((tm, FOX_DIM), row), pl.BlockSpec((tm, FOX_DIM), row),
                   pl.BlockSpec((tm, FOX_DIM), row), pl.BlockSpec((tm, FOX_DIM), row),
                   pl.BlockSpec((tm, FOX_DIM), row), pl.BlockSpec((FOX_HEADS, tm), col),
                   pl.BlockSpec((tm, LANES), row), pl.BlockSpec((FOX_HEADS, tm), col)],
        out_shape=[tok(SC_DIM, F32), tok(SC_DIM, BF16), tok(FOX_DIM, BF16), tok(FOX_DIM, F32),
                   tok(FOX_DIM, BF16), tok(FOX_DIM, F32), tok(FOX_DIM, BF16),
                   jax.ShapeDtypeStruct((FOX_HEADS, t), F32), tok(LANES, F32),
                   jax.ShapeDtypeStruct((FOX_HEADS, t), F32)],
        scratch_shapes=[pltpu.VMEM((1, LANES), F32), pltpu.VMEM((FOX_HEADS, LANES), F32)],
        compiler_params=_params("arbitrary"),
        name="inproj_odd",
    )(x, g[None, :], wu, wq, wk, wv, wf, wfr.T.astype(BF16), fb, f_bias.astype(F32)[:, None],
      jnp.tile(q_norm_g.astype(F32), FOX_HEADS)[None, :], jnp.tile(k_norm_g.astype(F32), FOX_HEADS)[None, :],
      bd, _tri(tm, block=block), _tri(tm, upper=True, block=block))


SC_HALO = 8


def _sconv_kernel(cu_ref, gb_ref, st_ref, w_ref, o_ref, buf, *, tl):
    i = pl.program_id(1)
    keep = SC_WIDTH - 1
    lo = SC_HALO - keep

    @pl.when(i == 0)
    def _():
        buf[lo:SC_HALO, :] = st_ref[0]

    @pl.when(i > 0)
    def _():
        buf[lo:SC_HALO, :] = buf[tl + lo:tl + SC_HALO, :]

    buf[SC_HALO:SC_HALO + tl, :] = cu_ref[0]
    conv = w_ref[0:1, :] * buf[lo:lo + tl, :]
    for k in range(1, SC_WIDTH):
        conv = conv + w_ref[k:k + 1, :] * buf[lo + k:lo + k + tl, :]
    o_ref[0] = (gb_ref[0].astype(F32) * conv).astype(BF16)


def _sconv(cu, gb, state, w):
    bsz, seqlen, _ = cu.shape
    tl = min(TOKEN_TILE, seqlen)
    seq = lambda bi, i: (bi, i, 0)
    return pl.pallas_call(
        functools.partial(_sconv_kernel, tl=tl),
        grid=(bsz, seqlen // tl),
        in_specs=[pl.BlockSpec((1, tl, SC_DIM), seq), pl.BlockSpec((1, tl, SC_DIM), seq),
                  pl.BlockSpec((1, SC_WIDTH - 1, SC_DIM), lambda bi, i: (bi, 0, 0)),
                  _full((SC_WIDTH, SC_DIM))],
        out_specs=pl.BlockSpec((1, tl, SC_DIM), seq),
        out_shape=jax.ShapeDtypeStruct((bsz, seqlen, SC_DIM), BF16),
        scratch_shapes=[pltpu.VMEM((SC_HALO + tl, SC_DIM), F32)],
        compiler_params=_params("parallel", "arbitrary"),
        name="sconv",
    )(cu, gb, state, w)


def _softmax_step(s, v, m, l, acc):
    m_new = jnp.maximum(m, jnp.max(s, axis=-1, keepdims=True))
    alpha = jnp.exp(m - m_new)
    p = jnp.exp(s - m_new)
    l = alpha * l + jnp.sum(p, axis=-1, keepdims=True)
    acc = alpha * acc + _dot(p.astype(BF16), v)
    return m_new, l, acc


def _fox_prompt_kernel(q_ref, k_ref, v_ref, fc_ref, fr_ref, o_ref, *, tq):
    i = pl.program_id(1)
    row_id = lax.broadcasted_iota(jnp.int32, (tq, tq), 0)
    col_id = lax.broadcasted_iota(jnp.int32, (tq, tq), 1)
    causal = col_id <= row_id
    for hd in range(FOX_HEADS):
        sl = slice(hd * FOX_HEAD_DIM, (hd + 1) * FOX_HEAD_DIM)
        qh = q_ref[0, :, sl]
        fq = fc_ref[0, :, hd:hd + 1]

        def scores(j):
            r = pl.ds(pl.multiple_of(j * tq, tq), tq)
            s = _dot_nt(qh, k_ref[0, r, sl])
            return s + fq - fr_ref[0, j, hd:hd + 1, :], v_ref[0, r, sl]

        def body(j, carry):
            s, vj = scores(j)
            return _softmax_step(s, vj, *carry)

        init = (jnp.full((tq, 1), -jnp.inf, F32), jnp.zeros((tq, 1), F32),
                jnp.zeros((tq, FOX_HEAD_DIM), F32))
        carry = lax.fori_loop(0, i, body, init)
        s, vj = scores(i)
        _, l, acc = _softmax_step(jnp.where(causal, s, -jnp.inf), vj, *carry)
        o_ref[0, :, sl] = (acc / l).astype(BF16)


def _fox_prompt(q, k, v, f_col, f_row):
    bsz, seqlen, _ = q.shape
    tq = min(TOKEN_TILE, seqlen)
    nq = seqlen // tq
    fr = f_row.reshape(FOX_HEADS, bsz, nq, tq).transpose(1, 2, 0, 3)
    per_b = lambda bi, i: (bi, 0, 0)
    return pl.pallas_call(
        functools.partial(_fox_prompt_kernel, tq=tq),
        grid=(bsz, nq),
        in_specs=[pl.BlockSpec((1, tq, FOX_DIM), lambda bi, i: (bi, i, 0)),
                  pl.BlockSpec((1, seqlen, FOX_DIM), per_b), pl.BlockSpec((1, seqlen, FOX_DIM), per_b),
                  pl.BlockSpec((1, tq, LANES), lambda bi, i: (bi, i, 0)),
                  pl.BlockSpec((1, nq, FOX_HEADS, tq), lambda bi, i: (bi, 0, 0, 0))],
        out_specs=pl.BlockSpec((1, tq, FOX_DIM), lambda bi, i: (bi, i, 0)),
        out_shape=jax.ShapeDtypeStruct((bsz, seqlen, FOX_DIM), BF16),
        compiler_params=_params("parallel", "arbitrary"),
        name="fox_prompt",
    )(q, k, v, f_col, fr)


PAST_BLOCK = 512


def _fox_sample_kernel(q_ref, kn_ref, vn_ref, pk_ref, pv_ref, plr_ref, fc_ref, fr_ref, triu_ref, o_ref,
                       *, seqlen, past_len):
    nblk = past_len // PAST_BLOCK
    run = []
    carry = jnp.zeros((FOX_HEADS, 1), F32)
    for c in range(nblk):
        cum = _cumsum_lanes(plr_ref[0, :, c * PAST_BLOCK:(c + 1) * PAST_BLOCK], triu_ref[...]) + carry
        carry = cum[:, PAST_BLOCK - 1:PAST_BLOCK]
        run.append(cum)
    tail = [carry - cum for cum in run]
    row_id = lax.broadcasted_iota(jnp.int32, (seqlen, seqlen), 0)
    col_id = lax.broadcasted_iota(jnp.int32, (seqlen, seqlen), 1)
    causal = col_id <= row_id
    for hd in range(FOX_HEADS):
        sl = slice(hd * FOX_HEAD_DIM, (hd + 1) * FOX_HEAD_DIM)
        qh = q_ref[0, :, sl]
        fq = fc_ref[0, :, hd:hd + 1]
        carry_sm = (jnp.full((seqlen, 1), -jnp.inf, F32), jnp.zeros((seqlen, 1), F32),
                    jnp.zeros((seqlen, FOX_HEAD_DIM), F32))
        for c in range(nblk):
            r = slice(c * PAST_BLOCK, (c + 1) * PAST_BLOCK)
            s = _dot_nt(qh, pk_ref[0, r, sl].astype(BF16)) + fq + tail[c][hd:hd + 1, :]
            carry_sm = _softmax_step(s, pv_ref[0, r, sl].astype(BF16), *carry_sm)
        s = _dot_nt(qh, kn_ref[0, :, sl]) + fq - fr_ref[0, hd:hd + 1, :]
        _, l, acc = _softmax_step(jnp.where(causal, s, -jnp.inf), vn_ref[0, :, sl], *carry_sm)
        o_ref[0, :, sl] = (acc / l).astype(BF16)


def _fox_sample(q, k, v, past_k, past_v, past_logf, f_col, f_row):
    bsz, seqlen, _ = q.shape
    past_len = past_k.shape[1]
    plr = past_logf.astype(F32).transpose(0, 2, 1)
    fr = f_row.reshape(FOX_HEADS, bsz, seqlen).transpose(1, 0, 2)
    per_b = lambda bi: (bi, 0, 0)
    new = pl.BlockSpec((1, seqlen, FOX_DIM), per_b)
    past = pl.BlockSpec((1, past_len, FOX_DIM), per_b)
    return pl.pallas_call(
        functools.partial(_fox_sample_kernel, seqlen=seqlen, past_len=past_len),
        grid=(bsz,),
        in_specs=[new, new, new, past, past, pl.BlockSpec((1, FOX_HEADS, past_len), per_b),
                  pl.BlockSpec((1, seqlen, LANES), per_b), pl.BlockSpec((1, FOX_HEADS, seqlen), per_b),
                  _full((PAST_BLOCK, PAST_BLOCK))],
        out_specs=new,
        out_shape=jax.ShapeDtypeStruct((bsz, seqlen, FOX_DIM), BF16),
        compiler_params=_params("parallel"),
        name="fox_sample",
    )(q, k, v, past_k, past_v, plr, f_col, fr, _tri(PAST_BLOCK, upper=True))


def _memkv_kernel(m_ref, g_ref, wk_ref, wv_ref, kg_ref, k_ref, v_ref):
    hm = _rms(m_ref[...], g_ref[0]).astype(BF16)
    k = _dot(hm, wk_ref[0])
    for hd in range(MEM_HEADS):
        sl = slice(hd * MEM_HEAD_DIM, (hd + 1) * MEM_HEAD_DIM)
        k_ref[0, :, sl] = _rms(k[:, sl], kg_ref[0])
    v_ref[0] = _dot(hm, wv_ref[0])


def _mem_kv(mem, norm_g, wk, wv, k_norm_g):
    rows = mem.shape[0]
    depth = norm_g.shape[0]
    tm = min(TOKEN_TILE, rows)
    per_l = lambda l, i: (l, 0, 0)
    out = pl.BlockSpec((1, tm, MEM_DIM), lambda l, i: (l, i, 0))
    return pl.pallas_call(
        _memkv_kernel,
        grid=(depth, rows // tm),
        in_specs=[pl.BlockSpec((tm, D_MODEL), lambda l, i: (i, 0)),
                  pl.BlockSpec((1, 1, D_MODEL), per_l),
                  pl.BlockSpec((1, D_MODEL, MEM_DIM), per_l), pl.BlockSpec((1, D_MODEL, MEM_DIM), per_l),
                  pl.BlockSpec((1, 1, MEM_HEAD_DIM), per_l)],
        out_specs=[out, out],
        out_shape=[jax.ShapeDtypeStruct((depth, rows, MEM_DIM), F32)] * 2,
        compiler_params=_params("parallel", "parallel"),
        name="mem_kv",
    )(mem, norm_g[:, None, :], wk.astype(BF16), wv.astype(BF16), k_norm_g[:, None, :])


def _xattn_kernel(y_ref, g_ref, wq_ref, qg_ref, mk_ref, mv_ref, wo_ref, o_ref, q_scr, a_scr, *, nb, tb):
    y = y_ref[...]
    q = _dot(_rms(y, g_ref[...]).astype(BF16), wq_ref[...])
    for hd in range(MEM_HEADS):
        sl = slice(hd * MEM_HEAD_DIM, (hd + 1) * MEM_HEAD_DIM)
        q_scr[:, sl] = _rms(q[:, sl], qg_ref[...]).astype(BF16)

    def one_sequence(bi, carry):
        r = pl.ds(pl.multiple_of(bi * tb, tb), tb)
        for hd in range(MEM_HEADS):
            sl = slice(hd * MEM_HEAD_DIM, (hd + 1) * MEM_HEAD_DIM)
            s = _dot_nt(q_scr[r, sl], mk_ref[bi, :, sl].astype(BF16)) * (MEM_HEAD_DIM ** -0.5)
            e = jnp.exp(s - jnp.max(s, axis=-1, keepdims=True))
            o = _dot(e.astype(BF16), mv_ref[bi, :, sl].astype(BF16)) / jnp.sum(e, axis=-1, keepdims=True)
            a_scr[r, sl] = o.astype(BF16)
        return carry

    lax.fori_loop(0, nb, one_sequence, 0)
    o_ref[...] = y + _dot(a_scr[...], wo_ref[...])


def _cross_attn(y, g, wq, q_norm_g, mk, mv, wo, seqlen):
    t = y.shape[0]
    if seqlen >= TOKEN_TILE:
        tm, nb, tb = TOKEN_TILE, 1, TOKEN_TILE
        per_seq = seqlen // tm
        mem_idx = lambda i: (i // per_seq, 0, 0)
    else:
        tm = min(TOKEN_TILE // 2, t)
        nb, tb = tm // seqlen, seqlen
        mem_idx = lambda i: (i, 0, 0)
    row = lambda i: (i, 0)
    mem = pl.BlockSpec((nb, MEM_TOKENS, MEM_DIM), mem_idx)
    return pl.pallas_call(
        functools.partial(_xattn_kernel, nb=nb, tb=tb),
        grid=(t // tm,),
        in_specs=[pl.BlockSpec((tm, D_MODEL), row), _full((1, D_MODEL)), _full((D_MODEL, MEM_DIM)),
                  _full((1, MEM_HEAD_DIM)), mem, mem, _full((MEM_DIM, D_MODEL))],
        out_specs=pl.BlockSpec((tm, D_MODEL), row),
        out_shape=jax.ShapeDtypeStruct((t, D_MODEL), F32),
        scratch_shapes=[pltpu.VMEM((tm, MEM_DIM), BF16), pltpu.VMEM((tm, MEM_DIM), BF16)],
        compiler_params=_params("parallel"),
        name="cross_attn",
    )(y, g[None, :], wq.astype(BF16), q_norm_g[None, :], mk, mv, wo.astype(BF16))


def _ffn_kernel(y_ref, g_ref, w1_ref, w3_ref, w2_ref, o_ref):
    y = y_ref[...]
    h = _rms(y, g_ref[...]).astype(BF16)
    u = (_silu(_dot(h, w1_ref[...])) * _dot(h, w3_ref[...])).astype(BF16)
    o_ref[...] = y + _dot(u, w2_ref[...])


def _ffn(y, g, w1, w3, w2):
    t = y.shape[0]
    tm = min(TOKEN_TILE, t)
    row = lambda i: (i, 0)
    once = pl.Buffered(1)
    res = lambda shape: pl.BlockSpec(shape, lambda i: (0, 0), pipeline_mode=once)
    return pl.pallas_call(
        _ffn_kernel,
        grid=(t // tm,),
        in_specs=[pl.BlockSpec((tm, D_MODEL), row), _full((1, D_MODEL)),
                  res(w1.shape), res(w3.shape), res(w2.shape)],
        out_specs=pl.BlockSpec((tm, D_MODEL), row),
        out_shape=jax.ShapeDtypeStruct((t, D_MODEL), F32),
        compiler_params=_params("parallel"),
        name="ffn",
    )(y, g[None, :], w1.astype(BF16), w3.astype(BF16), w2.astype(BF16))


def _last_rows(prev, x, n):
    return jnp.concatenate([prev.astype(x.dtype), x], axis=1)[:, -n:]


def _mixer_even(y, bsz, seqlen, conf_state, ssd_conv_state, ssd_state, norm_g, w_in, conf_w, conf_b,
                ln_g, ln_b, xbc_w, xbc_b, dt_bias, a_log, d_skip, gnorm_g, w_out):
    glu, zs, xbc, dtf, dtr = _inproj_even(y, norm_g, w_in, dt_bias)
    seq = lambda a: a.reshape(bsz, seqlen, a.shape[-1])
    glu, xbc = seq(glu), seq(xbc)
    a_out = _conf_conv(glu, conf_state, conf_w, conf_b, ln_g, ln_b)
    h0t = ssd_state.reshape(bsz, SSD_DIM, SSD_STATE).transpose(0, 2, 1)
    dtr = dtr.reshape(SSD_HEADS, bsz, seqlen).transpose(1, 0, 2)
    b_out, ht = _ssd(xbc, seq(dtf), dtr, seq(zs), ssd_conv_state, h0t, xbc_w, xbc_b, a_log, d_skip, gnorm_g)
    y = _outproj(y, a_out.reshape(-1, CONF_DIM), b_out.reshape(-1, SSD_DIM), w_out)
    new_ssd = ht.transpose(0, 2, 1).reshape(bsz, SSD_HEADS, SSD_HEAD_DIM, SSD_STATE)
    return (y, _last_rows(conf_state, glu, CONF_WIDTH - 1), _last_rows(ssd_conv_state, xbc, SSD_CONV - 1),
            new_ssd)


def _mixer_odd(y, bsz, seqlen, sconv_state, past, norm_g, w_in, sconv_w, q_norm_g, k_norm_g, f_bias, w_out):
    cu, gb, q, kf, kb, vf, vb, lfr, f_col, f_row = _inproj_odd(y, norm_g, w_in, q_norm_g, k_norm_g, f_bias,
                                                               seqlen)
    seq = lambda a: a.reshape(bsz, seqlen, a.shape[-1])
    cu = seq(cu)
    c_out = _sconv(cu, seq(gb), sconv_state, sconv_w)
    if past is None:
        d_out = _fox_prompt(seq(q), seq(kb), seq(vb), seq(f_col), f_row)
    else:
        past_k, past_v, past_logf = past
        d_out = _fox_sample(seq(q), seq(kb), seq(vb), past_k.reshape(bsz, -1, FOX_DIM),
                            past_v.reshape(bsz, -1, FOX_DIM), past_logf, seq(f_col), f_row)
    y = _outproj(y, c_out.reshape(-1, SC_DIM), d_out.reshape(-1, FOX_DIM), w_out)
    heads = lambda a: a.reshape(bsz, seqlen, FOX_HEADS, FOX_HEAD_DIM)
    logf = lfr.reshape(FOX_HEADS, bsz, seqlen).transpose(1, 2, 0)
    return y, _last_rows(sconv_state, cu, SC_WIDTH - 1), heads(kf), heads(vf), logf


def kernel(x_prompt, x_sample, state_conf_conv, state_ssd_conv, state_ssd, state_sconv, cache_fox_k, cache_fox_v, cache_fox_logf, cache_mem_k, cache_mem_v, mem_prompt, norm_mix_g, norm_cross_g, norm_mem_g, norm_ffn_g, ev_w_in, ev_w_out, conf_dw_w, conf_dw_b, conf_ln_g, conf_ln_b, ssd_conv_w, ssd_conv_b, ssd_dt_bias, ssd_a_log, ssd_d, ssd_norm_g, od_w_in, od_w_out, sconv_w, fox_q_norm_g, fox_k_norm_g, fox_f_bias, xa_wq, xa_wk, xa_wv, xa_wo, xa_q_norm_g, xa_k_norm_g, ffn_w1, ffn_w3, ffn_w2):
    bp, lp, _ = x_prompt.shape
    bs, ls, _ = x_sample.shape
    depth = norm_mix_g.shape[0]
    dt_ = x_prompt.dtype
    yp = x_prompt.reshape(bp * lp, D_MODEL)
    ys = x_sample.reshape(bs * ls, D_MODEL)
    mem_k, mem_v = _mem_kv(mem_prompt.reshape(-1, D_MODEL), norm_mem_g, xa_wk, xa_wv, xa_k_norm_g)
    mem_k = mem_k.reshape(depth, bp, MEM_TOKENS, MEM_DIM)
    mem_v = mem_v.reshape(depth, bp, MEM_TOKENS, MEM_DIM)
    outs_p = {k: [] for k in ("conf", "xbc", "ssd", "sc", "fk", "fv", "fl")}
    outs_s = {k: [] for k in ("conf", "xbc", "ssd", "sc", "fk", "fv", "fl")}
    for l in range(depth):
        g = l // 2
        if l % 2 == 0:
            w_ev = (norm_mix_g[l], ev_w_in[g], conf_dw_w[g], conf_dw_b[g], conf_ln_g[g], conf_ln_b[g],
                    ssd_conv_w[g], ssd_conv_b[g], ssd_dt_bias[g], ssd_a_log[g], ssd_d[g], ssd_norm_g[g],
                    ev_w_out[g])
            yp, c1, c2, c3 = _mixer_even(yp, bp, lp, jnp.zeros((bp, CONF_WIDTH - 1, CONF_DIM), dt_),
                                         jnp.zeros((bp, SSD_CONV - 1, SSD_XBC), dt_),
                                         jnp.zeros((bp, SSD_HEADS, SSD_HEAD_DIM, SSD_STATE), dt_), *w_ev)
            ys, s1, s2, s3 = _mixer_even(ys, bs, ls, state_conf_conv[g], state_ssd_conv[g], state_ssd[g], *w_ev)
            for d, vals in ((outs_p, (c1, c2, c3)), (outs_s, (s1, s2, s3))):
                for key, val in zip(("conf", "xbc", "ssd"), vals):
                    d[key].append(val)
        else:
            w_od = (norm_mix_g[l], od_w_in[g], sconv_w[g], fox_q_norm_g[g], fox_k_norm_g[g], fox_f_bias[g],
                    od_w_out[g])
            yp, c1, c2, c3, c4 = _mixer_odd(yp, bp, lp, jnp.zeros((bp, SC_WIDTH - 1, SC_DIM), dt_), None, *w_od)
            ys, s1, s2, s3, s4 = _mixer_odd(ys, bs, ls, state_sconv[g],
                                            (cache_fox_k[g], cache_fox_v[g], cache_fox_logf[g]), *w_od)
            for d, vals in ((outs_p, (c1, c2, c3, c4)), (outs_s, (s1, s2, s3, s4))):
                for key, val in zip(("sc", "fk", "fv", "fl"), vals):
                    d[key].append(val)
        yp = _cross_attn(yp, norm_cross_g[l], xa_wq[l], xa_q_norm_g[l], mem_k[l], mem_v[l], xa_wo[l], lp)
        ys = _cross_attn(ys, norm_cross_g[l], xa_wq[l], xa_q_norm_g[l],
                         cache_mem_k[l].reshape(bs, MEM_TOKENS, MEM_DIM),
                         cache_mem_v[l].reshape(bs, MEM_TOKENS, MEM_DIM), xa_wo[l], ls)
        yp = _ffn(yp, norm_ffn_g[l], ffn_w1[l], ffn_w3[l], ffn_w2[l])
        ys = _ffn(ys, norm_ffn_g[l], ffn_w1[l], ffn_w3[l], ffn_w2[l])
    st = jnp.stack
    mem_heads = lambda a: a.reshape(depth, bp, MEM_TOKENS, MEM_HEADS, MEM_HEAD_DIM)
    return (yp.reshape(bp, lp, D_MODEL), ys.reshape(bs, ls, D_MODEL),
            st(outs_p["conf"]), st(outs_s["conf"]), st(outs_p["xbc"]), st(outs_s["xbc"]),
            st(outs_p["ssd"]), st(outs_s["ssd"]), st(outs_p["sc"]), st(outs_s["sc"]),
            st(outs_p["fk"]), st(outs_s["fk"]), st(outs_p["fv"]), st(outs_s["fv"]),
            st(outs_p["fl"]), st(outs_s["fl"]), mem_heads(mem_k), mem_heads(mem_v))
```

```python
import functools

import jax
import jax.numpy as jnp
from jax import lax
from jax.experimental import pallas as pl
from jax.experimental.pallas import tpu as pltpu

F32 = jnp.float32
BF16 = jnp.bfloat16

D_MODEL = 1024
RMS_EPS = 1e-6
LN_EPS = 1e-5
CONF_DIM = 512
CONF_WIDTH = 31
SSD_HEADS = 8
SSD_HEAD_DIM = 64
SSD_DIM = 512
SSD_STATE = 128
SSD_GROUPS = 2
SSD_CONV = 4
SSD_XBC = 1024
SC_DIM = 512
SC_WIDTH = 3
FOX_HEADS = 8
FOX_HEAD_DIM = 64
FOX_DIM = 512
MEM_TOKENS = 256
MEM_HEADS = 4
MEM_HEAD_DIM = 128
MEM_DIM = 512

LOG2E = 1.4426950408889634
LANES = 128
SUBLANES = 8
TOKEN_TILE = 512
SSD_Q = 128
VMEM_LIMIT = 56 * 1024 * 1024


def _params(*sem):
    return pltpu.CompilerParams(dimension_semantics=sem, vmem_limit_bytes=VMEM_LIMIT)


def _full(shape):
    nd = len(shape)
    return pl.BlockSpec(shape, lambda *_: (0,) * nd)


def _rms(x, g):
    return x * lax.rsqrt(jnp.mean(x * x, axis=-1, keepdims=True) + RMS_EPS) * g


def _softplus(x):
    return jnp.maximum(x, 0.0) + jnp.log1p(jnp.exp(-jnp.abs(x)))


def _sigmoid(x):
    return 1.0 / (1.0 + jnp.exp(-x))


def _silu(x):
    return x * _sigmoid(x)


def _dot(a, b):
    return jnp.dot(a, b, preferred_element_type=F32)


def _dot_nt(a, b):
    return lax.dot_general(a, b, (((1,), (1,)), ((), ())), preferred_element_type=F32)


def _dot_tn(a, b):
    return lax.dot_general(a, b, (((0,), (0,)), ((), ())), preferred_element_type=F32)


def _split3(x):
    hi = x.astype(BF16)
    r = x - hi.astype(F32)
    mid = r.astype(BF16)
    lo = (r - mid.astype(F32)).astype(BF16)
    return hi, mid, lo


def _cumsum_rows(tril, x):
    hi, mid, lo = _split3(x)
    return _dot(tril, hi) + _dot(tril, mid) + _dot(tril, lo)


def _cumsum_lanes(x, triu):
    hi, mid, lo = _split3(x)
    return _dot(hi, triu) + _dot(mid, triu) + _dot(lo, triu)


def _tri(n, upper=False, block=None):
    r = jnp.arange(n)
    m = (r[:, None] <= r[None, :]) if upper else (r[:, None] >= r[None, :])
    if block is not None:
        m = m & ((r[:, None] // block) == (r[None, :] // block))
    return m.astype(BF16)


def _inproj_even_kernel(x_ref, g_ref, wa_ref, wz_ref, wx_ref, wd_ref, wdt_ref, db_ref, dbc_ref,
                        glu_ref, zs_ref, xbc_ref, dtf_ref, dtr_ref):
    h = _rms(x_ref[...], g_ref[...]).astype(BF16)
    a = _dot(h, wa_ref[...])
    glu_ref[...] = a[:, :CONF_DIM] * _sigmoid(a[:, CONF_DIM:])
    zs_ref[...] = _silu(_dot(h, wz_ref[...])).astype(BF16)
    xbc_ref[...] = _dot(h, wx_ref[...])
    dtf_ref[...] = _softplus(_dot(h, wd_ref[...]) + db_ref[...])
    dtr_ref[...] = _softplus(_dot_nt(wdt_ref[...], h) + dbc_ref[...])


def _inproj_even(x, g, w_in, dt_bias):
    t = x.shape[0]
    tm = min(TOKEN_TILE, t)
    wa = w_in[:, :2 * CONF_DIM].astype(BF16)
    wz = w_in[:, 2 * CONF_DIM:2 * CONF_DIM + SSD_DIM].astype(BF16)
    wx = w_in[:, 2 * CONF_DIM + SSD_DIM:2 * CONF_DIM + SSD_DIM + SSD_XBC].astype(BF16)
    wdt = w_in[:, 2 * CONF_DIM + SSD_DIM + SSD_XBC:]
    wd = jnp.repeat(wdt, SSD_HEAD_DIM, axis=1).astype(BF16)
    db = jnp.repeat(dt_bias.astype(F32), SSD_HEAD_DIM)[None, :]
    row = lambda i: (i, 0)
    return pl.pallas_call(
        _inproj_even_kernel,
        grid=(t // tm,),
        in_specs=[pl.BlockSpec((tm, D_MODEL), row), _full((1, D_MODEL)),
                  _full(wa.shape), _full(wz.shape), _full(wx.shape), _full(wd.shape),
                  _full((SSD_HEADS, D_MODEL)), _full((1, SSD_DIM)), _full((SSD_HEADS, 1))],
        out_specs=[pl.BlockSpec((tm, CONF_DIM), row), pl.BlockSpec((tm, SSD_DIM), row),
                   pl.BlockSpec((tm, SSD_XBC), row), pl.BlockSpec((tm, SSD_DIM), row),
                   pl.BlockSpec((SSD_HEADS, tm), lambda i: (0, i))],
        out_shape=[jax.ShapeDtypeStruct((t, CONF_DIM), F32), jax.ShapeDtypeStruct((t, SSD_DIM), BF16),
                   jax.ShapeDtypeStruct((t, SSD_XBC), F32), jax.ShapeDtypeStruct((t, SSD_DIM), F32),
                   jax.ShapeDtypeStruct((SSD_HEADS, t), F32)],
        compiler_params=_params("parallel"),
        name="inproj_even",
    )(x, g[None, :], wa, wz, wx, wd, wdt.T.astype(BF16), db, dt_bias.astype(F32)[:, None])


CONF_HALO = 32
CONF_ROWS = 32


def _conf_kernel(x_ref, st_ref, w_ref, b_ref, lg_ref, lb_ref, o_ref, buf, shifted, *, tl):
    i = pl.program_id(1)
    keep = CONF_WIDTH - 1
    lo = CONF_HALO - keep

    @pl.when(i == 0)
    def _():
        buf[lo:CONF_HALO, :] = st_ref[0]

    @pl.when(i > 0)
    def _():
        buf[lo:CONF_HALO, :] = buf[tl + lo:tl + CONF_HALO, :]

    buf[CONF_HALO:CONF_HALO + tl, :] = x_ref[0]
    n = CONF_HALO + tl - SUBLANES
    for s in range(1, SUBLANES):
        shifted[s - 1] = buf[s:s + n, :]
    rows = min(CONF_ROWS, tl)
    groups = rows // SUBLANES
    for c in range(tl // rows):
        accs = [b_ref[...]] * groups
        for k in range(CONF_WIDTH):
            a, s = divmod(lo + k, SUBLANES)
            wk = w_ref[k]
            for gi in range(groups):
                r0 = c * rows + (a + gi) * SUBLANES
                tap = buf[r0:r0 + SUBLANES, :] if s == 0 else shifted[s - 1, r0:r0 + SUBLANES, :]
                accs[gi] = accs[gi] + wk * tap
        acc = jnp.concatenate(accs, axis=0)
        mu = jnp.mean(acc, axis=-1, keepdims=True)
        xc = acc - mu
        var = jnp.mean(xc * xc, axis=-1, keepdims=True)
        y = xc * lax.rsqrt(var + LN_EPS) * lg_ref[...] + lb_ref[...]
        o_ref[0, c * rows:(c + 1) * rows, :] = _silu(y).astype(BF16)


def _conf_conv(glu, state, w, b, ln_g, ln_b):
    bsz, seqlen, _ = glu.shape
    tl = min(TOKEN_TILE, seqlen)
    return pl.pallas_call(
        functools.partial(_conf_kernel, tl=tl),
        grid=(bsz, seqlen // tl),
        in_specs=[pl.BlockSpec((1, tl, CONF_DIM), lambda bi, i: (bi, i, 0)),
                  pl.BlockSpec((1, CONF_WIDTH - 1, CONF_DIM), lambda bi, i: (bi, 0, 0)),
                  _full((CONF_WIDTH, SUBLANES, CONF_DIM)), _full((SUBLANES, CONF_DIM)), _full((1, CONF_DIM)),
                  _full((1, CONF_DIM))],
        out_specs=pl.BlockSpec((1, tl, CONF_DIM), lambda bi, i: (bi, i, 0)),
        out_shape=jax.ShapeDtypeStruct((bsz, seqlen, CONF_DIM), BF16),
        scratch_shapes=[pltpu.VMEM((CONF_HALO + tl, CONF_DIM), F32),
                        pltpu.VMEM((SUBLANES - 1, CONF_HALO + tl - SUBLANES, CONF_DIM), F32)],
        compiler_params=_params("parallel", "arbitrary"),
        name="conf_conv",
    )(glu, state, jnp.broadcast_to(w[:, None, :], (CONF_WIDTH, SUBLANES, CONF_DIM)),
      jnp.broadcast_to(b[None, :], (SUBLANES, CONF_DIM)), ln_g[None, :], ln_b[None, :])


SSD_HALO = 8


def _ssd_kernel(xbc_ref, dtf_ref, dtr_ref, zs_ref, cst_ref, h0_ref, cw_ref, cb_ref, af_ref, ac_ref,
                df_ref, gn_ref, tril_ref, triu_ref, o_ref, ht_ref, xbuf, state, y_scr, *, q, nq):
    i = pl.program_id(1)
    tl = q * nq
    keep = SSD_CONV - 1
    lo = SSD_HALO - keep
    gw = SSD_DIM // SSD_GROUPS
    hpg = SSD_HEADS // SSD_GROUPS

    @pl.when(i == 0)
    def _():
        xbuf[lo:SSD_HALO, :] = cst_ref[0]
        state[...] = h0_ref[0]

    @pl.when(i > 0)
    def _():
        xbuf[lo:SSD_HALO, :] = xbuf[tl + lo:tl + SSD_HALO, :]

    xbuf[SSD_HALO:SSD_HALO + tl, :] = xbc_ref[0]
    row_id = lax.broadcasted_iota(jnp.int32, (q, q), 0)
    col_id = lax.broadcasted_iota(jnp.int32, (q, q), 1)
    causal = row_id >= col_id

    for c in range(nq):
        r0 = c * q
        conv = jnp.broadcast_to(cb_ref[...], (q, SSD_XBC))
        for k in range(SSD_CONV):
            conv = conv + cw_ref[k:k + 1, :] * xbuf[lo + r0 + k:lo + r0 + k + q, :]
        xc = _silu(conv)
        xs = xc[:, :SSD_DIM]
        b_all = xc[:, SSD_DIM:SSD_DIM + SSD_GROUPS * SSD_STATE].astype(BF16)
        c_all = xc[:, SSD_DIM + SSD_GROUPS * SSD_STATE:].astype(BF16)
        dtf = dtf_ref[0, r0:r0 + q, :]
        cs = _cumsum_rows(tril_ref[...], dtf * af_ref[...])
        cs_row = _cumsum_lanes(dtr_ref[0, :, r0:r0 + q] * ac_ref[...], triu_ref[...])
        cs_last = cs[q - 1:q, :]
        xdt = xs * dtf
        xdt_b = xdt.astype(BF16)
        xdtd = (xdt * jnp.exp(cs_last - cs)).astype(BF16)
        chunk_decay = jnp.exp(cs_last)
        for g in range(SSD_GROUPS):
            bg = b_all[:, g * SSD_STATE:(g + 1) * SSD_STATE]
            cg = c_all[:, g * SSD_STATE:(g + 1) * SSD_STATE]
            cb = _dot_nt(cg, bg)
            st_g = state[:, g * gw:(g + 1) * gw]
            y_off = _dot(cg, st_g.astype(BF16))
            for hh in range(hpg):
                hd = g * hpg + hh
                sl = slice(hd * SSD_HEAD_DIM, (hd + 1) * SSD_HEAD_DIM)
                seg = cs[:, hd * SSD_HEAD_DIM:hd * SSD_HEAD_DIM + 1] - cs_row[hd:hd + 1, :]
                lmat = jnp.exp(jnp.where(causal, seg, -jnp.inf))
                y_scr[:, sl] = _dot((cb * lmat).astype(BF16), xdt_b[:, sl])
            gs = slice(g * gw, (g + 1) * gw)
            y_scr[:, gs] = y_scr[:, gs] + y_off * jnp.exp(cs[:, gs])
            state[:, gs] = st_g * chunk_decay[:, gs] + _dot_tn(bg, xdtd[:, gs])
        y = y_scr[...] + df_ref[...] * xs
        yg = y * zs_ref[0, r0:r0 + q, :].astype(F32)
        for g in range(SSD_GROUPS):
            gs = slice(g * gw, (g + 1) * gw)
            o_ref[0, r0:r0 + q, gs] = _rms(yg[:, gs], gn_ref[:, gs]).astype(BF16)

    @pl.when(i == pl.num_programs(1) - 1)
    def _():
        ht_ref[0] = state[...]


def _ssd(xbc, dtf, dtr, zs, conv_state, h0t, conv_w, conv_b, a_log, d_skip, gnorm_g):
    bsz, seqlen, _ = xbc.shape
    q = min(SSD_Q, seqlen)
    tl = min(TOKEN_TILE, seqlen)
    nq = tl // q
    a = -jnp.exp(a_log.astype(F32))
    af = jnp.repeat(a, SSD_HEAD_DIM)[None, :]
    df = jnp.repeat(d_skip.astype(F32), SSD_HEAD_DIM)[None, :]
    seq = lambda bi, i: (bi, i, 0)
    per_b = lambda bi, i: (bi, 0, 0)
    return pl.pallas_call(
        functools.partial(_ssd_kernel, q=q, nq=nq),
        grid=(bsz, seqlen // tl),
        in_specs=[pl.BlockSpec((1, tl, SSD_XBC), seq), pl.BlockSpec((1, tl, SSD_DIM), seq),
                  pl.BlockSpec((1, SSD_HEADS, tl), lambda bi, i: (bi, 0, i)),
                  pl.BlockSpec((1, tl, SSD_DIM), seq),
                  pl.BlockSpec((1, SSD_CONV - 1, SSD_XBC), per_b),
                  pl.BlockSpec((1, SSD_STATE, SSD_DIM), per_b),
                  _full((SSD_CONV, SSD_XBC)), _full((1, SSD_XBC)), _full((1, SSD_DIM)),
                  _full((SSD_HEADS, 1)), _full((1, SSD_DIM)), _full((1, SSD_DIM)),
                  _full((q, q)), _full((q, q))],
        out_specs=[pl.BlockSpec((1, tl, SSD_DIM), seq), pl.BlockSpec((1, SSD_STATE, SSD_DIM), per_b)],
        out_shape=[jax.ShapeDtypeStruct((bsz, seqlen, SSD_DIM), BF16),
                   jax.ShapeDtypeStruct((bsz, SSD_STATE, SSD_DIM), F32)],
        scratch_shapes=[pltpu.VMEM((SSD_HALO + tl, SSD_XBC), F32),
                        pltpu.VMEM((SSD_STATE, SSD_DIM), F32),
                        pltpu.VMEM((q, SSD_DIM), F32)],
        compiler_params=_params("parallel", "arbitrary"),
        name="ssd",
    )(xbc, dtf, dtr, zs, conv_state, h0t, conv_w, conv_b[None, :], af, a[:, None], df,
      gnorm_g[None, :], _tri(q), _tri(q, upper=True))


def _outproj_kernel(x_ref, a_ref, b_ref, wa_ref, wb_ref, o_ref):
    o_ref[...] = x_ref[...] + _dot(a_ref[...], wa_ref[...]) + _dot(b_ref[...], wb_ref[...])


def _outproj(x, a, b, w_out):
    t = x.shape[0]
    tm = min(TOKEN_TILE, t)
    ka = a.shape[1]
    wa = w_out[:ka].astype(BF16)
    wb = w_out[ka:].astype(BF16)
    row = lambda i: (i, 0)
    return pl.pallas_call(
        _outproj_kernel,
        grid=(t // tm,),
        in_specs=[pl.BlockSpec((tm, D_MODEL), row), pl.BlockSpec((tm, ka), row),
                  pl.BlockSpec((tm, b.shape[1]), row), _full(wa.shape), _full(wb.shape)],
        out_specs=pl.BlockSpec((tm, D_MODEL), row),
        out_shape=jax.ShapeDtypeStruct((t, D_MODEL), F32),
        compiler_params=_params("parallel"),
        name="outproj",
    )(x, a, b, wa, wb)


def _inproj_odd_kernel(x_ref, g_ref, wu_ref, wq_ref, wk_ref, wv_ref, wf_ref, wft_ref, fb_ref, fbc_ref,
                       qg_ref, kg_ref, bd_ref, tril_ref, triu_ref,
                       cu_ref, gb_ref, q_ref, kf_ref, kb_ref, vf_ref, vb_ref, lfr_ref, fc_ref, fr_ref,
                       carry_c, carry_r, *, tiles_per_seq):
    i = pl.program_id(0)

    @pl.when(i % tiles_per_seq == 0)
    def _():
        carry_c[...] = jnp.zeros_like(carry_c)
        carry_r[...] = jnp.zeros_like(carry_r)

    h = _rms(x_ref[...], g_ref[...]).astype(BF16)
    ugg = _dot(h, wu_ref[...])
    cu_ref[...] = ugg[:, 2 * SC_DIM:] * ugg[:, :SC_DIM]
    gb_ref[...] = ugg[:, SC_DIM:2 * SC_DIM].astype(BF16)

    tm = x_ref.shape[0]

    def head_norm(v, gain):
        ms = _dot((v * v).astype(BF16), bd_ref[...]) * (1.0 / FOX_HEAD_DIM)
        return v * lax.rsqrt(ms + RMS_EPS) * gain

    def store_heads(ref, val):
        for hd in range(FOX_HEADS):
            ref[pl.ds(hd, tm, stride=FOX_HEADS), :] = val[:, hd * FOX_HEAD_DIM:(hd + 1) * FOX_HEAD_DIM]

    qn = head_norm(_dot(h, wq_ref[...]), qg_ref[...])
    q_ref[...] = (qn * (FOX_HEAD_DIM ** -0.5 * LOG2E)).astype(BF16)
    kn = head_norm(_dot(h, wk_ref[...]), kg_ref[...])
    store_heads(kf_ref, kn)
    kb_ref[...] = kn.astype(BF16)
    v = _dot(h, wv_ref[...])
    store_heads(vf_ref, v)
    vb_ref[...] = v.astype(BF16)

    lf_col = -_softplus(-(_dot(h, wf_ref[...]) + fb_ref[...]))
    lf_row = -_softplus(-(_dot_nt(wft_ref[...], h) + fbc_ref[...]))
    lfr_ref[...] = lf_row
    f_col = _cumsum_rows(tril_ref[...], lf_col) + carry_c[...]
    f_row = _cumsum_lanes(lf_row, triu_ref[...]) + carry_r[:, 0:1]
    fc_ref[...] = f_col * LOG2E
    fr_ref[...] = f_row * LOG2E
    carry_c[...] = f_col[tm - 1:tm, :]
    carry_r[...] = jnp.broadcast_to(f_row[:, tm - 1:tm], carry_r.shape)


def _inproj_odd(x, g, w_in, q_norm_g, k_norm_g, f_bias, seqlen):
    t = x.shape[0]
    tm = min(TOKEN_TILE, t)
    tiles_per_seq = max(seqlen // tm, 1)
    block = seqlen if seqlen < tm else None
    wu = w_in[:, :3 * SC_DIM].astype(BF16)
    wq = w_in[:, 3 * SC_DIM:3 * SC_DIM + FOX_DIM].astype(BF16)
    wk = w_in[:, 3 * SC_DIM + FOX_DIM:3 * SC_DIM + 2 * FOX_DIM].astype(BF16)
    wv = w_in[:, 3 * SC_DIM + 2 * FOX_DIM:3 * SC_DIM + 3 * FOX_DIM].astype(BF16)
    wfr = w_in[:, 3 * SC_DIM + 3 * FOX_DIM:]
    wf = jnp.pad(wfr, ((0, 0), (0, LANES - FOX_HEADS))).astype(BF16)
    fb = jnp.pad(f_bias.astype(F32), (0, LANES - FOX_HEADS))[None, :]
    hid = jnp.arange(FOX_DIM) // FOX_HEAD_DIM
    bd = (hid[:, None] == hid[None, :]).astype(BF16)
    row = lambda i: (i, 0)
    col = lambda i: (0, i)
    tok = lambda n, dt: jax.ShapeDtypeStruct((t, n), dt)
    heads_major = jax.ShapeDtypeStruct((t * FOX_HEADS, FOX_HEAD_DIM), F32)
    return pl.pallas_call(
        functools.partial(_inproj_odd_kernel, tiles_per_seq=tiles_per_seq),
        grid=(t // tm,),
        in_specs=[pl.BlockSpec((tm, D_MODEL), row), _full((1, D_MODEL)),
                  _full(wu.shape), _full(wq.shape), _full(wk.shape), _full(wv.shape), _full(wf.shape),
                  _full((FOX_HEADS, D_MODEL)), _full((1, LANES)), _full((FOX_HEADS, 1)),
                  _full((1, FOX_DIM)), _full((1, FOX_DIM)), _full((FOX_DIM, FOX_DIM)),
                  _full((tm, tm)), _full((tm, tm))],
        out_specs=[pl.BlockSpec((tm, SC_DIM), row), pl.BlockSpec((tm, SC_DIM), row),
                   pl.BlockSpec((tm, FOX_DIM), row), pl.BlockSpec((tm * FOX_HEADS, FOX_HEAD_DIM), row),
                   pl.BlockSpec((tm, FOX_DIM), row), pl.BlockSpec((tm * FOX_HEADS, FOX_HEAD_DIM), row),
                   pl.BlockSpec((tm, FOX_DIM), row), pl.BlockSpec((FOX_HEADS, tm), col),
                   pl.BlockSpec((tm, LANES), row), pl.BlockSpec((FOX_HEADS, tm), col)],
        out_shape=[tok(SC_DIM, F32), tok(SC_DIM, BF16), tok(FOX_DIM, BF16), heads_major,
                   tok(FOX_DIM, BF16), heads_major, tok(FOX_DIM, BF16),
                   jax.ShapeDtypeStruct((FOX_HEADS, t), F32), tok(LANES, F32),
                   jax.ShapeDtypeStruct((FOX_HEADS, t), F32)],
        scratch_shapes=[pltpu.VMEM((1, LANES), F32), pltpu.VMEM((FOX_HEADS, LANES), F32)],
        compiler_params=_params("arbitrary"),
        name="inproj_odd",
    )(x, g[None, :], wu, wq, wk, wv, wf, wfr.T.astype(BF16), fb, f_bias.astype(F32)[:, None],
      jnp.tile(q_norm_g.astype(F32), FOX_HEADS)[None, :], jnp.tile(k_norm_g.astype(F32), FOX_HEADS)[None, :],
      bd, _tri(tm, block=block), _tri(tm, upper=True, block=block))


SC_HALO = 8


def _sconv_kernel(cu_ref, gb_ref, st_ref, w_ref, o_ref, buf, *, tl):
    i = pl.program_id(1)
    keep = SC_WIDTH - 1
    lo = SC_HALO - keep

    @pl.when(i == 0)
    def _():
        buf[lo:SC_HALO, :] = st_ref[0]

    @pl.when(i > 0)
    def _():
        buf[lo:SC_HALO, :] = buf[tl + lo:tl + SC_HALO, :]

    buf[SC_HALO:SC_HALO + tl, :] = cu_ref[0]
    conv = w_ref[0:1, :] * buf[lo:lo + tl, :]
    for k in range(1, SC_WIDTH):
        conv = conv + w_ref[k:k + 1, :] * buf[lo + k:lo + k + tl, :]
    o_ref[0] = (gb_ref[0].astype(F32) * conv).astype(BF16)


def _sconv(cu, gb, state, w):
    bsz, seqlen, _ = cu.shape
    tl = min(TOKEN_TILE, seqlen)
    seq = lambda bi, i: (bi, i, 0)
    return pl.pallas_call(
        functools.partial(_sconv_kernel, tl=tl),
        grid=(bsz, seqlen // tl),
        in_specs=[pl.BlockSpec((1, tl, SC_DIM), seq), pl.BlockSpec((1, tl, SC_DIM), seq),
                  pl.BlockSpec((1, SC_WIDTH - 1, SC_DIM), lambda bi, i: (bi, 0, 0)),
                  _full((SC_WIDTH, SC_DIM))],
        out_specs=pl.BlockSpec((1, tl, SC_DIM), seq),
        out_shape=jax.ShapeDtypeStruct((bsz, seqlen, SC_DIM), BF16),
        scratch_shapes=[pltpu.VMEM((SC_HALO + tl, SC_DIM), F32)],
        compiler_params=_params("parallel", "arbitrary"),
        name="sconv",
    )(cu, gb, state, w)


HEAD_PAIRS = FOX_HEADS // 2


def _fox_prompt_kernel(q_ref, k_ref, v_ref, fc_ref, fr_ref, o_ref, *, tq):
    i = pl.program_id(1)
    nk = tq // LANES
    lane = lax.broadcasted_iota(jnp.int32, (tq, LANES), 1)
    low_half = lane < FOX_HEAD_DIM
    halves = (low_half, ~low_half)
    fq = [jnp.broadcast_to(fc_ref[0, :, hd:hd + 1], (tq, LANES)) for hd in range(FOX_HEADS)]

    def block(j, state, diagonal):
        r = pl.ds(pl.multiple_of(j * tq, tq), tq)
        out = []
        for hp in range(HEAD_PAIRS):
            pair = slice(hp * LANES, (hp + 1) * LANES)
            q_pair = q_ref[0, :, pair]
            k_pair = k_ref[0, r, pair]
            v_pair = v_ref[0, r, pair]
            for half in range(2):
                hd = 2 * hp + half
                m_old, acc = state[hd]
                qh = jnp.where(halves[half], q_pair, jnp.zeros_like(q_pair))
                vh = jnp.where(halves[half], v_pair, jnp.ones_like(v_pair))
                t = _dot_nt(qh, k_pair) - fr_ref[0, j, hd:hd + 1, :]
                if diagonal:
                    row_id = lax.broadcasted_iota(jnp.int32, (tq, tq), 0)
                    col_id = lax.broadcasted_iota(jnp.int32, (tq, tq), 1)
                    t = jnp.where(col_id <= row_id, t, -jnp.inf)
                chunks = [t[:, c * LANES:(c + 1) * LANES] for c in range(nk)]
                mx = functools.reduce(jnp.maximum, chunks)
                mx = jnp.broadcast_to(jnp.max(mx, axis=-1, keepdims=True), (tq, LANES))
                m_new = jnp.maximum(m_old, mx + fq[hd])
                shift = m_new - fq[hd]
                p = jnp.concatenate([jnp.exp2(c - shift).astype(BF16) for c in chunks], axis=1)
                acc = jnp.exp2(m_old - m_new) * acc + _dot(p, vh)
                out.append((m_new, acc))
        return tuple(out)

    init = tuple((jnp.full((tq, LANES), -jnp.inf, F32), jnp.zeros((tq, LANES), F32))
                 for _ in range(FOX_HEADS))
    state = lax.fori_loop(0, i, lambda j, s: block(j, s, False), init)
    state = block(i, state, True)
    for hp in range(HEAD_PAIRS):
        acc_lo, acc_hi = state[2 * hp][1], state[2 * hp + 1][1]
        num = jnp.where(low_half, acc_lo, acc_hi)
        den = pltpu.roll(jnp.where(low_half, acc_hi, acc_lo), FOX_HEAD_DIM, 1)
        o_ref[0, :, hp * LANES:(hp + 1) * LANES] = (num / den).astype(BF16)


def _fox_prompt(q, k, v, f_col, f_row):
    bsz, seqlen, _ = q.shape
    tq = min(TOKEN_TILE, seqlen)
    nq = seqlen // tq
    fr = f_row.reshape(FOX_HEADS, bsz, nq, tq).transpose(1, 2, 0, 3)
    per_b = lambda bi, i: (bi, 0, 0)
    return pl.pallas_call(
        functools.partial(_fox_prompt_kernel, tq=tq),
        grid=(bsz, nq),
        in_specs=[pl.BlockSpec((1, tq, FOX_DIM), lambda bi, i: (bi, i, 0)),
                  pl.BlockSpec((1, seqlen, FOX_DIM), per_b), pl.BlockSpec((1, seqlen, FOX_DIM), per_b),
                  pl.BlockSpec((1, tq, LANES), lambda bi, i: (bi, i, 0)),
                  pl.BlockSpec((1, nq, FOX_HEADS, tq), lambda bi, i: (bi, 0, 0, 0))],
        out_specs=pl.BlockSpec((1, tq, FOX_DIM), lambda bi, i: (bi, i, 0)),
        out_shape=jax.ShapeDtypeStruct((bsz, seqlen, FOX_DIM), BF16),
        compiler_params=_params("parallel", "arbitrary"),
        name="fox_prompt",
    )(q, k, v, f_col, fr)


PAST_BLOCK = 512


def _fox_sample_kernel(q_ref, kn_ref, vn_ref, pk_ref, pv_ref, plr_ref, fc_ref, fr_ref, triu_ref, o_ref,
                       *, seqlen, past_len):
    nblk = past_len // PAST_BLOCK
    run = []
    carry = jnp.zeros((FOX_HEADS, 1), F32)
    for c in range(nblk):
        cum = _cumsum_lanes(plr_ref[0, :, c * PAST_BLOCK:(c + 1) * PAST_BLOCK], triu_ref[...]) + carry
        carry = cum[:, PAST_BLOCK - 1:PAST_BLOCK]
        run.append(cum)
    tail = jnp.concatenate([(carry - cum) * LOG2E for cum in run], axis=1)
    row_id = lax.broadcasted_iota(jnp.int32, (seqlen, seqlen), 0)
    col_id = lax.broadcasted_iota(jnp.int32, (seqlen, seqlen), 1)
    causal = col_id <= row_id
    for hd in range(FOX_HEADS):
        sl = slice(hd * FOX_HEAD_DIM, (hd + 1) * FOX_HEAD_DIM)
        rows = pl.ds(hd, past_len, stride=FOX_HEADS)
        qh = q_ref[0, :, sl]
        fq = fc_ref[0, :, hd:hd + 1]
        t_past = _dot_nt(qh, pk_ref[0, rows, :].astype(BF16)) + tail[hd:hd + 1, :]
        t_new = jnp.where(causal, _dot_nt(qh, kn_ref[0, :, sl]) - fr_ref[0, hd:hd + 1, :], -jnp.inf)
        m = jnp.maximum(jnp.max(t_past, axis=-1, keepdims=True), jnp.max(t_new, axis=-1, keepdims=True)) + fq
        p_past = jnp.exp2(t_past - (m - fq))
        p_new = jnp.exp2(t_new - (m - fq))
        l = jnp.sum(p_past, axis=-1, keepdims=True) + jnp.sum(p_new, axis=-1, keepdims=True)
        acc = (_dot(p_past.astype(BF16), pv_ref[0, rows, :].astype(BF16))
               + _dot(p_new.astype(BF16), vn_ref[0, :, sl]))
        o_ref[0, :, sl] = (acc / l).astype(BF16)


def _fox_sample(q, k, v, past_k, past_v, past_logf, f_col, f_row):
    bsz, seqlen, _ = q.shape
    past_len = past_logf.shape[1]
    plr = past_logf.astype(F32).transpose(0, 2, 1)
    fr = f_row.reshape(FOX_HEADS, bsz, seqlen).transpose(1, 0, 2)
    per_b = lambda bi: (bi, 0, 0)
    new = pl.BlockSpec((1, seqlen, FOX_DIM), per_b)
    past = pl.BlockSpec((1, past_len * FOX_HEADS, FOX_HEAD_DIM), per_b)
    return pl.pallas_call(
        functools.partial(_fox_sample_kernel, seqlen=seqlen, past_len=past_len),
        grid=(bsz,),
        in_specs=[new, new, new, past, past, pl.BlockSpec((1, FOX_HEADS, past_len), per_b),
                  pl.BlockSpec((1, seqlen, LANES), per_b), pl.BlockSpec((1, FOX_HEADS, seqlen), per_b),
                  _full((PAST_BLOCK, PAST_BLOCK))],
        out_specs=new,
        out_shape=jax.ShapeDtypeStruct((bsz, seqlen, FOX_DIM), BF16),
        compiler_params=_params("parallel"),
        name="fox_sample",
    )(q, k, v, past_k, past_v, plr, f_col, fr, _tri(PAST_BLOCK, upper=True))


def _memkv_kernel(m_ref, g_ref, wk_ref, wv_ref, kg_ref, k_ref, v_ref):
    hm = _rms(m_ref[...], g_ref[0]).astype(BF16)
    k = _dot(hm, wk_ref[0])
    for hd in range(MEM_HEADS):
        sl = slice(hd * MEM_HEAD_DIM, (hd + 1) * MEM_HEAD_DIM)
        k_ref[0, :, sl] = _rms(k[:, sl], kg_ref[0])
    v_ref[0] = _dot(hm, wv_ref[0])


def _mem_kv(mem, norm_g, wk, wv, k_norm_g):
    rows = mem.shape[0]
    depth = norm_g.shape[0]
    tm = min(TOKEN_TILE, rows)
    per_l = lambda l, i: (l, 0, 0)
    out = pl.BlockSpec((1, tm, MEM_DIM), lambda l, i: (l, i, 0))
    return pl.pallas_call(
        _memkv_kernel,
        grid=(depth, rows // tm),
        in_specs=[pl.BlockSpec((tm, D_MODEL), lambda l, i: (i, 0)),
                  pl.BlockSpec((1, 1, D_MODEL), per_l),
                  pl.BlockSpec((1, D_MODEL, MEM_DIM), per_l), pl.BlockSpec((1, D_MODEL, MEM_DIM), per_l),
                  pl.BlockSpec((1, 1, MEM_HEAD_DIM), per_l)],
        out_specs=[out, out],
        out_shape=[jax.ShapeDtypeStruct((depth, rows, MEM_DIM), F32)] * 2,
        compiler_params=_params("parallel", "parallel"),
        name="mem_kv",
    )(mem, norm_g[:, None, :], wk.astype(BF16), wv.astype(BF16), k_norm_g[:, None, :])


def _xattn_kernel(y_ref, g_ref, wq_ref, qg_ref, mk_ref, mv_ref, wo_ref, o_ref, q_scr, a_scr, *, nb, tb):
    y = y_ref[...]
    q = _dot(_rms(y, g_ref[...]).astype(BF16), wq_ref[...])
    for hd in range(MEM_HEADS):
        sl = slice(hd * MEM_HEAD_DIM, (hd + 1) * MEM_HEAD_DIM)
        q_scr[:, sl] = _rms(q[:, sl], qg_ref[...]).astype(BF16)

    def one_sequence(bi, carry):
        r = pl.ds(pl.multiple_of(bi * tb, tb), tb)
        for hd in range(MEM_HEADS):
            sl = slice(hd * MEM_HEAD_DIM, (hd + 1) * MEM_HEAD_DIM)
            s = _dot_nt(q_scr[r, sl], mk_ref[bi, :, sl].astype(BF16)) * (MEM_HEAD_DIM ** -0.5)
            e = jnp.exp(s - jnp.max(s, axis=-1, keepdims=True))
            o = _dot(e.astype(BF16), mv_ref[bi, :, sl].astype(BF16)) / jnp.sum(e, axis=-1, keepdims=True)
            a_scr[r, sl] = o.astype(BF16)
        return carry

    lax.fori_loop(0, nb, one_sequence, 0)
    o_ref[...] = y + _dot(a_scr[...], wo_ref[...])


def _cross_attn(y, g, wq, q_norm_g, mk, mv, wo, seqlen):
    t = y.shape[0]
    if seqlen >= TOKEN_TILE:
        tm, nb, tb = TOKEN_TILE, 1, TOKEN_TILE
        per_seq = seqlen // tm
        mem_idx = lambda i: (i // per_seq, 0, 0)
    else:
        tm = min(TOKEN_TILE // 2, t)
        nb, tb = tm // seqlen, seqlen
        mem_idx = lambda i: (i, 0, 0)
    row = lambda i: (i, 0)
    mem = pl.BlockSpec((nb, MEM_TOKENS, MEM_DIM), mem_idx)
    return pl.pallas_call(
        functools.partial(_xattn_kernel, nb=nb, tb=tb),
        grid=(t // tm,),
        in_specs=[pl.BlockSpec((tm, D_MODEL), row), _full((1, D_MODEL)), _full((D_MODEL, MEM_DIM)),
                  _full((1, MEM_HEAD_DIM)), mem, mem, _full((MEM_DIM, D_MODEL))],
        out_specs=pl.BlockSpec((tm, D_MODEL), row),
        out_shape=jax.ShapeDtypeStruct((t, D_MODEL), F32),
        scratch_shapes=[pltpu.VMEM((tm, MEM_DIM), BF16), pltpu.VMEM((tm, MEM_DIM), BF16)],
        compiler_params=_params("parallel"),
        name="cross_attn",
    )(y, g[None, :], wq.astype(BF16), q_norm_g[None, :], mk, mv, wo.astype(BF16))


def _ffn_kernel(y_ref, g_ref, w1_ref, w3_ref, w2_ref, o_ref):
    y = y_ref[...]
    h = _rms(y, g_ref[...]).astype(BF16)
    u = (_silu(_dot(h, w1_ref[...])) * _dot(h, w3_ref[...])).astype(BF16)
    o_ref[...] = y + _dot(u, w2_ref[...])


def _ffn(y, g, w1, w3, w2):
    t = y.shape[0]
    tm = min(TOKEN_TILE, t)
    row = lambda i: (i, 0)
    once = pl.Buffered(1)
    res = lambda shape: pl.BlockSpec(shape, lambda i: (0, 0), pipeline_mode=once)
    return pl.pallas_call(
        _ffn_kernel,
        grid=(t // tm,),
        in_specs=[pl.BlockSpec((tm, D_MODEL), row), _full((1, D_MODEL)),
                  res(w1.shape), res(w3.shape), res(w2.shape)],
        out_specs=pl.BlockSpec((tm, D_MODEL), row),
        out_shape=jax.ShapeDtypeStruct((t, D_MODEL), F32),
        compiler_params=_params("parallel"),
        name="ffn",
    )(y, g[None, :], w1.astype(BF16), w3.astype(BF16), w2.astype(BF16))


def _last_rows(prev, x, n):
    return jnp.concatenate([prev.astype(x.dtype), x], axis=1)[:, -n:]


def _mixer_even(y, bsz, seqlen, conf_state, ssd_conv_state, ssd_state, norm_g, w_in, conf_w, conf_b,
                ln_g, ln_b, xbc_w, xbc_b, dt_bias, a_log, d_skip, gnorm_g, w_out):
    glu, zs, xbc, dtf, dtr = _inproj_even(y, norm_g, w_in, dt_bias)
    seq = lambda a: a.reshape(bsz, seqlen, a.shape[-1])
    glu, xbc = seq(glu), seq(xbc)
    a_out = _conf_conv(glu, conf_state, conf_w, conf_b, ln_g, ln_b)
    h0t = ssd_state.reshape(bsz, SSD_DIM, SSD_STATE).transpose(0, 2, 1)
    dtr = dtr.reshape(SSD_HEADS, bsz, seqlen).transpose(1, 0, 2)
    b_out, ht = _ssd(xbc, seq(dtf), dtr, seq(zs), ssd_conv_state, h0t, xbc_w, xbc_b, a_log, d_skip, gnorm_g)
    y = _outproj(y, a_out.reshape(-1, CONF_DIM), b_out.reshape(-1, SSD_DIM), w_out)
    new_ssd = ht.transpose(0, 2, 1).reshape(bsz, SSD_HEADS, SSD_HEAD_DIM, SSD_STATE)
    return (y, _last_rows(conf_state, glu, CONF_WIDTH - 1), _last_rows(ssd_conv_state, xbc, SSD_CONV - 1),
            new_ssd)


def _mixer_odd(y, bsz, seqlen, sconv_state, past, norm_g, w_in, sconv_w, q_norm_g, k_norm_g, f_bias, w_out):
    cu, gb, q, kf, kb, vf, vb, lfr, f_col, f_row = _inproj_odd(y, norm_g, w_in, q_norm_g, k_norm_g, f_bias,
                                                               seqlen)
    seq = lambda a: a.reshape(bsz, seqlen, a.shape[-1])
    cu = seq(cu)
    c_out = _sconv(cu, seq(gb), sconv_state, sconv_w)
    if past is None:
        d_out = _fox_prompt(seq(q), seq(kb), seq(vb), seq(f_col), f_row)
    else:
        past_k, past_v, past_logf = past
        d_out = _fox_sample(seq(q), seq(kb), seq(vb), past_k.reshape(bsz, -1, FOX_HEAD_DIM),
                            past_v.reshape(bsz, -1, FOX_HEAD_DIM), past_logf, seq(f_col), f_row)
    y = _outproj(y, c_out.reshape(-1, SC_DIM), d_out.reshape(-1, FOX_DIM), w_out)
    heads = lambda a: a.reshape(bsz, seqlen, FOX_HEADS, FOX_HEAD_DIM)
    logf = lfr.reshape(FOX_HEADS, bsz, seqlen).transpose(1, 2, 0)
    return y, _last_rows(sconv_state, cu, SC_WIDTH - 1), heads(kf), heads(vf), logf


def kernel(x_prompt, x_sample, state_conf_conv, state_ssd_conv, state_ssd, state_sconv, cache_fox_k, cache_fox_v, cache_fox_logf, cache_mem_k, cache_mem_v, mem_prompt, norm_mix_g, norm_cross_g, norm_mem_g, norm_ffn_g, ev_w_in, ev_w_out, conf_dw_w, conf_dw_b, conf_ln_g, conf_ln_b, ssd_conv_w, ssd_conv_b, ssd_dt_bias, ssd_a_log, ssd_d, ssd_norm_g, od_w_in, od_w_out, sconv_w, fox_q_norm_g, fox_k_norm_g, fox_f_bias, xa_wq, xa_wk, xa_wv, xa_wo, xa_q_norm_g, xa_k_norm_g, ffn_w1, ffn_w3, ffn_w2):
    bp, lp, _ = x_prompt.shape
    bs, ls, _ = x_sample.shape
    depth = norm_mix_g.shape[0]
    dt_ = x_prompt.dtype
    yp = x_prompt.reshape(bp * lp, D_MODEL)
    ys = x_sample.reshape(bs * ls, D_MODEL)
    mem_k, mem_v = _mem_kv(mem_prompt.reshape(-1, D_MODEL), norm_mem_g, xa_wk, xa_wv, xa_k_norm_g)
    mem_k = mem_k.reshape(depth, bp, MEM_TOKENS, MEM_DIM)
    mem_v = mem_v.reshape(depth, bp, MEM_TOKENS, MEM_DIM)
    outs_p = {k: [] for k in ("conf", "xbc", "ssd", "sc", "fk", "fv", "fl")}
    outs_s = {k: [] for k in ("conf", "xbc", "ssd", "sc", "fk", "fv", "fl")}
    for l in range(depth):
        g = l // 2
        if l % 2 == 0:
            w_ev = (norm_mix_g[l], ev_w_in[g], conf_dw_w[g], conf_dw_b[g], conf_ln_g[g], conf_ln_b[g],
                    ssd_conv_w[g], ssd_conv_b[g], ssd_dt_bias[g], ssd_a_log[g], ssd_d[g], ssd_norm_g[g],
                    ev_w_out[g])
            yp, c1, c2, c3 = _mixer_even(yp, bp, lp, jnp.zeros((bp, CONF_WIDTH - 1, CONF_DIM), dt_),
                                         jnp.zeros((bp, SSD_CONV - 1, SSD_XBC), dt_),
                                         jnp.zeros((bp, SSD_HEADS, SSD_HEAD_DIM, SSD_STATE), dt_), *w_ev)
            ys, s1, s2, s3 = _mixer_even(ys, bs, ls, state_conf_conv[g], state_ssd_conv[g], state_ssd[g], *w_ev)
            for d, vals in ((outs_p, (c1, c2, c3)), (outs_s, (s1, s2, s3))):
                for key, val in zip(("conf", "xbc", "ssd"), vals):
                    d[key].append(val)
        else:
            w_od = (norm_mix_g[l], od_w_in[g], sconv_w[g], fox_q_norm_g[g], fox_k_norm_g[g], fox_f_bias[g],
                    od_w_out[g])
            yp, c1, c2, c3, c4 = _mixer_odd(yp, bp, lp, jnp.zeros((bp, SC_WIDTH - 1, SC_DIM), dt_), None, *w_od)
            ys, s1, s2, s3, s4 = _mixer_odd(ys, bs, ls, state_sconv[g],
                                            (cache_fox_k[g], cache_fox_v[g], cache_fox_logf[g]), *w_od)
            for d, vals in ((outs_p, (c1, c2, c3, c4)), (outs_s, (s1, s2, s3, s4))):
                for key, val in zip(("sc", "fk", "fv", "fl"), vals):
                    d[key].append(val)
        yp = _cross_attn(yp, norm_cross_g[l], xa_wq[l], xa_q_norm_g[l], mem_k[l], mem_v[l], xa_wo[l], lp)
        ys = _cross_attn(ys, norm_cross_g[l], xa_wq[l], xa_q_norm_g[l],
                         cache_mem_k[l].reshape(bs, MEM_TOKENS, MEM_DIM),
                         cache_mem_v[l].reshape(bs, MEM_TOKENS, MEM_DIM), xa_wo[l], ls)
        yp = _ffn(yp, norm_ffn_g[l], ffn_w1[l], ffn_w3[l], ffn_w2[l])
        ys = _ffn(ys, norm_ffn_g[l], ffn_w1[l], ffn_w3[l], ffn_w2[l])
    st = jnp.stack
    mem_heads = lambda a: a.reshape(depth, bp, MEM_TOKENS, MEM_HEADS, MEM_HEAD_DIM)
    return (yp.reshape(bp, lp, D_MODEL), ys.reshape(bs, ls, D_MODEL),
            st(outs_p["conf"]), st(outs_s["conf"]), st(outs_p["xbc"]), st(outs_s["xbc"]),
            st(outs_p["ssd"]), st(outs_s["ssd"]), st(outs_p["sc"]), st(outs_s["sc"]),
            st(outs_p["fk"]), st(outs_s["fk"]), st(outs_p["fv"]), st(outs_s["fv"]),
            st(outs_p["fl"]), st(outs_s["fl"]), mem_heads(mem_k), mem_heads(mem_v))
```
